```python
import math
import jax, jax.numpy as jnp
from jax import lax
import numpy as np

D_MODEL = 2048
BATCH = 2
SEQ = 4096
DEPTH = 1

HEAD_DIM = 128
N_ATTN_HEADS = 8
N_DELTA_HEADS = 8
ATTN_WIDTH = N_ATTN_HEADS * HEAD_DIM
DELTA_WIDTH = N_DELTA_HEADS * HEAD_DIM
MIX_WIDTH = ATTN_WIDTH + DELTA_WIDTH
DILATED_PATTERNS = ((128, 1), (512, 4), (2048, 16))
Q_BLOCK = 128
ROPE_THETA = 500000.0
ROPE_DIM = HEAD_DIM // 4
CONV_WIDTH = 4
CHUNK = 64
D_FF = 5632
NORM_EPS = 1e-6
N_MOD = 9
IN_PROJ_WIDTH = 3 * ATTN_WIDTH + 4 * DELTA_WIDTH + 2 * N_DELTA_HEADS

kernel_name = "hybrid_dilated_attn_gated_deltanet_macaron_layer"


def rms_norm(x, gain):
    xf = x.astype(jnp.float32)
    y = xf * lax.rsqrt(jnp.mean(xf * xf, axis=-1, keepdims=True) + NORM_EPS)
    return (y * gain.astype(jnp.float32)).astype(x.dtype)


def l2_norm(x):
    xf = x.astype(jnp.float32)
    return xf * lax.rsqrt(jnp.sum(xf * xf, axis=-1, keepdims=True) + NORM_EPS)


def modulate(x, gain, shift, scale):
    return rms_norm(x, gain) * (1.0 + scale[:, None, :]) + shift[:, None, :]


def swiglu(h, w_gate, w_up, w_down):
    return (jax.nn.silu(h @ w_gate) * (h @ w_up)) @ w_down


def partial_rope(x, positions):
    half = ROPE_DIM // 2
    inv_freq = ROPE_THETA ** (-jnp.arange(half, dtype=jnp.float32) / half)
    ang = positions.astype(jnp.float32)[..., None] * inv_freq
    cos = jnp.cos(ang)[:, :, None, :]
    sin = jnp.sin(ang)[:, :, None, :]
    x1 = x[..., :half].astype(jnp.float32)
    x2 = x[..., half:ROPE_DIM].astype(jnp.float32)
    rot = jnp.concatenate([x1 * cos - x2 * sin, x2 * cos + x1 * sin], axis=-1).astype(x.dtype)
    return jnp.concatenate([rot, x[..., ROPE_DIM:]], axis=-1)


def dilated_window_branch(q, k, v, dilation, span):
    B, Sp, H, Dh = q.shape
    L = Sp // dilation
    nblk = L // Q_BLOCK

    def residue_blocks(t):
        t = t.reshape(B, L, dilation, H, Dh).transpose(0, 2, 1, 3, 4)
        return t.reshape(B, dilation, nblk, Q_BLOCK, H, Dh)

    def with_previous_block(t):
        prev = jnp.pad(t[:, :, :-1], ((0, 0), (0, 0), (1, 0), (0, 0), (0, 0), (0, 0)))
        return jnp.concatenate([prev, t], axis=3)

    qb = residue_blocks(q)
    kw = with_previous_block(residue_blocks(k))
    vw = with_previous_block(residue_blocks(v))
    s = jnp.einsum('brnqhd,brnkhd->brnhqk', qb, kw,
                   preferred_element_type=jnp.float32) * (HEAD_DIM ** -0.5)
    qi = jnp.arange(Q_BLOCK)[:, None]
    kj = jnp.arange(2 * Q_BLOCK)[None, :]
    dist = qi + Q_BLOCK - kj
    key_index = jnp.arange(nblk)[:, None, None] * Q_BLOCK - Q_BLOCK + kj[None]
    mask = (dist >= 0) & (dist <= span) & (key_index >= 0)
    s = jnp.where(mask[None, None, :, None], s, -jnp.inf)
    m = jnp.max(s, axis=-1, keepdims=True)
    p = jnp.exp(s - m)
    denom = jnp.sum(p, axis=-1)
    o = jnp.einsum('brnhqk,brnkhd->brnqhd', p, vw.astype(jnp.float32))
    o = o / jnp.swapaxes(denom, 3, 4)[..., None]
    lse = jnp.swapaxes(m[..., 0] + jnp.log(denom), 3, 4)

    def to_sequence(t):
        t = t.reshape(B, dilation, L, *t.shape[4:])
        return jnp.swapaxes(t, 1, 2).reshape(B, Sp, *t.shape[3:])

    return to_sequence(o), to_sequence(lse)


def dilated_attention(q, k, v):
    B, S, H, Dh = q.shape
    unit = Q_BLOCK
    for _, d in DILATED_PATTERNS:
        unit = unit * d // math.gcd(unit, d * Q_BLOCK) if False else math.lcm(unit, d * Q_BLOCK)
    Sp = -(-S // unit) * unit
    pad = ((0, 0), (0, Sp - S), (0, 0), (0, 0))
    q, k, v = (jnp.pad(t, pad) for t in (q, k, v))
    outs, lses = [], []
    for window, dilation in DILATED_PATTERNS:
        o, lse = dilated_window_branch(q, k, v, dilation, window // dilation)
        outs.append(o)
        lses.append(lse)
    wts = jax.nn.softmax(jnp.stack(lses, axis=0), axis=0)
    o = jnp.einsum('pbsh,pbshd->bshd', wts, jnp.stack(outs, axis=0))
    return o[:, :S].astype(q.dtype)


def causal_depthwise_conv(x, w):
    C = x.shape[-1]
    y = lax.conv_general_dilated(x, w[:, None, :].astype(x.dtype), window_strides=(1,),
                                 padding=[(CONV_WIDTH - 1, 0)],
                                 dimension_numbers=('NWC', 'WIO', 'NWC'),
                                 feature_group_count=C)
    return jax.nn.silu(y)


def gated_delta_rule(q, k, v, g, beta):
    B, S, H, Dk = q.shape
    Dv = v.shape[-1]
    N = S // CHUNK
    f32 = jnp.float32

    def chunk_vec(t):
        return t.astype(f32).reshape(B, N, CHUNK, H, t.shape[-1]).transpose(0, 3, 1, 2, 4)

    def chunk_scalar(t):
        return t.astype(f32).reshape(B, N, CHUNK, H).transpose(0, 3, 1, 2)

    q = chunk_vec(q) * (Dk ** -0.5)
    k, v = chunk_vec(k), chunk_vec(v)
    g, beta = chunk_scalar(g), chunk_scalar(beta)
    gc = jnp.cumsum(g, axis=-1)
    causal = jnp.tril(jnp.ones((CHUNK, CHUNK), bool))
    strict = jnp.tril(jnp.ones((CHUNK, CHUNK), bool), k=-1)
    gamma = jnp.exp(jnp.where(causal, gc[..., :, None] - gc[..., None, :], -jnp.inf))
    kb = k * beta[..., None]
    a = jnp.where(strict, jnp.einsum('bhnid,bhnjd->bhnij', kb, k) * gamma, 0.0)
    eye = jnp.eye(CHUNK, dtype=f32)
    t_inv = lax.linalg.triangular_solve(eye + a, jnp.broadcast_to(eye, a.shape),
                                        left_side=True, lower=True, unit_diagonal=True)
    u = t_inv @ (v * beta[..., None])
    w = t_inv @ (kb * jnp.exp(gc)[..., None])
    q_decay = q * jnp.exp(gc)[..., None]
    k_tail = k * jnp.exp(gc[..., -1:] - gc)[..., None]
    chunk_decay = jnp.exp(gc[..., -1])
    intra = jnp.einsum('bhnid,bhnjd->bhnij', q, k) * gamma

    def step(state, xs):
        u_c, w_c, qd_c, kt_c, intra_c, dec_c = xs
        v_new = u_c - jnp.einsum('bhcd,bhde->bhce', w_c, state)
        o_c = (jnp.einsum('bhcd,bhde->bhce', qd_c, state)
               + jnp.einsum('bhij,bhje->bhie', intra_c, v_new))
        state = state * dec_c[..., None, None] + jnp.einsum('bhcd,bhce->bhde', kt_c, v_new)
        return state, o_c

    xs = tuple(jnp.moveaxis(t, 2, 0) for t in (u, w, q_decay, k_tail, intra, chunk_decay))
    _, o = lax.scan(step, jnp.zeros((B, H, Dk, Dv), f32), xs)
    return o.transpose(1, 0, 3, 2, 4).reshape(B, S, H, Dv)


def hybrid_mixer(h, positions, w_in, conv_w, q_norm, k_norm, a_log, dt_bias, delta_out_norm, w_out):
    B, S, _ = h.shape
    proj = h @ w_in
    cuts = [3 * ATTN_WIDTH, 3 * ATTN_WIDTH + 3 * DELTA_WIDTH,
            3 * ATTN_WIDTH + 4 * DELTA_WIDTH, 3 * ATTN_WIDTH + 4 * DELTA_WIDTH + N_DELTA_HEADS]
    attn_qkv, delta_qkv, z, a_in, b_in = jnp.split(proj, cuts, axis=-1)

    qa, ka, va = (t.reshape(B, S, N_ATTN_HEADS, HEAD_DIM) for t in jnp.split(attn_qkv, 3, axis=-1))
    qa = partial_rope(rms_norm(qa, q_norm), positions)
    ka = partial_rope(rms_norm(ka, k_norm), positions)
    oa = dilated_attention(qa, ka, va)

    dqkv = causal_depthwise_conv(delta_qkv, conv_w)
    qd, kd, vd = (t.reshape(B, S, N_DELTA_HEADS, HEAD_DIM) for t in jnp.split(dqkv, 3, axis=-1))
    g = -jnp.exp(a_log.astype(jnp.float32)) * jax.nn.softplus(
        a_in.astype(jnp.float32) + dt_bias.astype(jnp.float32))
    beta = jax.nn.sigmoid(b_in.astype(jnp.float32))
    od = gated_delta_rule(l2_norm(qd), l2_norm(kd), vd, g, beta).astype(h.dtype)
    od = rms_norm(od, delta_out_norm) * jax.nn.silu(z.reshape(B, S, N_DELTA_HEADS, HEAD_DIM))

    o = jnp.concatenate([oa.reshape(B, S, ATTN_WIDTH), od.reshape(B, S, DELTA_WIDTH)], axis=-1)
    return o @ w_out


def setup_inputs(seed: int = 0) -> dict:
    key = jax.random.key(seed)
    ks = jax.random.split(key, 24)
    f32 = jnp.float32
    D = D_MODEL

    def nrm(k, shape, scale):
        return jax.random.normal(k, shape, f32) * scale

    def gain(k, shape):
        return 1.0 + 0.02 * jax.random.normal(k, shape, f32)

    start = jax.random.randint(ks[2], (BATCH, 1), 0, 1024, dtype=jnp.int32)
    positions = start + jnp.arange(SEQ, dtype=jnp.int32)[None, :]
    dt = jnp.exp(jax.random.uniform(ks[14], (DEPTH, N_DELTA_HEADS), f32,
                                    math.log(1e-3), math.log(1e-1)))
    return {
        "x": nrm(ks[0], (BATCH, SEQ, D), 1.0),
        "c": nrm(ks[1], (BATCH, D), 1.0),
        "positions": positions,
        "w_ada": nrm(ks[3], (DEPTH, D, N_MOD * D), 0.5 * D ** -0.5),
        "b_ada": nrm(ks[4], (DEPTH, N_MOD * D), 0.01),
        "ffn1_norm": gain(ks[5], (DEPTH, D)),
        "ffn1_w_gate": nrm(ks[6], (DEPTH, D, D_FF), D ** -0.5),
        "ffn1_w_up": nrm(ks[7], (DEPTH, D, D_FF), D ** -0.5),
        "ffn1_w_down": nrm(ks[8], (DEPTH, D_FF, D), D_FF ** -0.5),
        "mix_norm": gain(ks[9], (DEPTH, D)),
        "w_in": nrm(ks[10], (DEPTH, D, IN_PROJ_WIDTH), D ** -0.5),
        "conv_w": nrm(ks[11], (DEPTH, CONV_WIDTH, 3 * DELTA_WIDTH), CONV_WIDTH ** -0.5),
        "q_norm": gain(ks[12], (DEPTH, HEAD_DIM)),
        "k_norm": gain(ks[13], (DEPTH, HEAD_DIM)),
        "a_log": jnp.log(jax.random.uniform(ks[15], (DEPTH, N_DELTA_HEADS), f32, 1.0, 16.0)),
        "dt_bias": dt + jnp.log(-jnp.expm1(-dt)),
        "delta_out_norm": gain(ks[16], (DEPTH, HEAD_DIM)),
        "w_out": nrm(ks[17], (DEPTH, MIX_WIDTH, D), MIX_WIDTH ** -0.5),
        "ffn2_norm": gain(ks[18], (DEPTH, D)),
        "ffn2_w_gate": nrm(ks[19], (DEPTH, D, D_FF), D ** -0.5),
        "ffn2_w_up": nrm(ks[20], (DEPTH, D, D_FF), D ** -0.5),
        "ffn2_w_down": nrm(ks[21], (DEPTH, D_FF, D), D_FF ** -0.5),
    }


def reference(x, c, positions, w_ada, b_ada, ffn1_norm, ffn1_w_gate, ffn1_w_up, ffn1_w_down,
              mix_norm, w_in, conv_w, q_norm, k_norm, a_log, dt_bias, delta_out_norm, w_out,
              ffn2_norm, ffn2_w_gate, ffn2_w_up, ffn2_w_down):
    c_act = jax.nn.silu(c)
    for l in range(DEPTH):
        mod = c_act @ w_ada[l] + b_ada[l]
        (sh1, sc1, gt1, sh2, sc2, gt2, sh3, sc3, gt3) = jnp.split(mod, N_MOD, axis=-1)
        h = modulate(x, ffn1_norm[l], sh1, sc1)
        x = x + 0.5 * gt1[:, None, :] * swiglu(h, ffn1_w_gate[l], ffn1_w_up[l], ffn1_w_down[l])
        h = modulate(x, mix_norm[l], sh2, sc2)
        x = x + gt2[:, None, :] * hybrid_mixer(h, positions, w_in[l], conv_w[l], q_norm[l], k_norm[l],
                                                a_log[l], dt_bias[l], delta_out_norm[l], w_out[l])
        h = modulate(x, ffn2_norm[l], sh3, sc3)
        x = x + 0.5 * gt3[:, None, :] * swiglu(h, ffn2_w_gate[l], ffn2_w_up[l], ffn2_w_down[l])
    return x
```

```python
import functools
import math

import jax
import jax.numpy as jnp
from jax import lax
from jax.experimental import pallas as pl
from jax.experimental.pallas import tpu as pltpu

F32 = jnp.float32
BF16 = jnp.bfloat16

HEAD_DIM = 128
N_ATTN_HEADS = 8
N_DELTA_HEADS = 8
ATTN_WIDTH = N_ATTN_HEADS * HEAD_DIM
DELTA_WIDTH = N_DELTA_HEADS * HEAD_DIM
MIX_WIDTH = ATTN_WIDTH + DELTA_WIDTH
DILATIONS = (1, 4, 16)
Q_BLOCK = 128
ROPE_THETA = 500000.0
ROPE_HALF = HEAD_DIM // 8
CONV_WIDTH = 4
CHUNK = 64
NORM_EPS = 1e-6
N_MOD = 9
MAIN_PROJ = 3 * ATTN_WIDTH + 4 * DELTA_WIDTH
LANES = 128
SUBLANES = 8
VMEM_LIMIT = 56 * 1024 * 1024
NEG_INF = float("-inf")


def _cparams(semantics):
    return pltpu.CompilerParams(dimension_semantics=semantics, vmem_limit_bytes=VMEM_LIMIT)


def _pick(n, candidates):
    for c in candidates:
        if n % c == 0:
            return c
    return n


def _adaln_kernel(c_ref, w_ref, b_ref, o_ref):
    c = c_ref[...]
    ca = (c * jax.nn.sigmoid(c)).astype(BF16)
    o_ref[...] = jnp.dot(ca, w_ref[...].astype(BF16), preferred_element_type=F32) + b_ref[...]


def _adaln(c8, w, b):
    d, n = w.shape
    tn = _pick(n, (1024, 512, 256, 128))
    return pl.pallas_call(
        _adaln_kernel,
        grid=(n // tn,),
        in_specs=[pl.BlockSpec((SUBLANES, d), lambda j: (0, 0)),
                  pl.BlockSpec((d, tn), lambda j: (0, j)),
                  pl.BlockSpec((1, tn), lambda j: (0, j))],
        out_specs=pl.BlockSpec((SUBLANES, tn), lambda j: (0, j)),
        out_shape=jax.ShapeDtypeStruct((SUBLANES, n), F32),
        compiler_params=_cparams(("parallel",)),
        name="adaln",
    )(c8, w, b)


def _modulated(x, gain, mod_ref, sub):
    ms = jnp.mean(x * x, axis=-1, keepdims=True)
    y = x * lax.rsqrt(ms + NORM_EPS) * gain
    return y * (1.0 + mod_ref[0, 3 * sub + 1:3 * sub + 2, :]) + mod_ref[0, 3 * sub:3 * sub + 1, :]


def _ffn_kernel(x_ref, mod_ref, gain_ref, wg_ref, wu_ref, wd_ref, o_ref, h_ref, *, sub, nf):
    f = pl.program_id(1)

    @pl.when(f == 0)
    def _():
        h_ref[...] = _modulated(x_ref[...], gain_ref[...], mod_ref, sub).astype(BF16)
        o_ref[...] = jnp.zeros_like(o_ref)

    h = h_ref[...]
    g = jnp.dot(h, wg_ref[...], preferred_element_type=F32)
    u = jnp.dot(h, wu_ref[...], preferred_element_type=F32)
    a = (g * jax.nn.sigmoid(g) * u).astype(BF16)
    o_ref[...] += jnp.dot(a, wd_ref[...], preferred_element_type=F32)

    @pl.when(f == nf - 1)
    def _():
        gate = 0.5 * mod_ref[0, 3 * sub + 2:3 * sub + 3, :]
        o_ref[...] = x_ref[...] + gate * o_ref[...]


def _ffn(x2d, mod, gain, wg, wu, wd, *, sub, seq):
    t, d = x2d.shape
    dff = wg.shape[1]
    tm = _pick(seq, (512, 256, 128))
    tf = _pick(dff, (512, 256, 128))
    nf = dff // tf
    return pl.pallas_call(
        functools.partial(_ffn_kernel, sub=sub, nf=nf),
        grid=(t // tm, nf),
        in_specs=[pl.BlockSpec((tm, d), lambda i, f: (i, 0)),
                  pl.BlockSpec((1, N_MOD, d), lambda i, f: ((i * tm) // seq, 0, 0)),
                  pl.BlockSpec((1, d), lambda i, f: (0, 0)),
                  pl.BlockSpec((d, tf), lambda i, f: (0, f)),
                  pl.BlockSpec((d, tf), lambda i, f: (0, f)),
                  pl.BlockSpec((tf, d), lambda i, f: (f, 0))],
        out_specs=pl.BlockSpec((tm, d), lambda i, f: (i, 0)),
        out_shape=jax.ShapeDtypeStruct((t, d), F32),
        scratch_shapes=[pltpu.VMEM((tm, d), BF16)],
        compiler_params=_cparams(("parallel", "arbitrary")),
        name=f"ffn{sub}",
    )(x2d, mod, gain, wg, wu, wd)


def _inproj_kernel(x_ref, mod_ref, gain_ref, w_ref, wab_ref, o_ref, ab_ref, h_ref):
    n = pl.program_id(1)

    @pl.when(n == 0)
    def _():
        h = _modulated(x_ref[...], gain_ref[...], mod_ref, 1).astype(BF16)
        h_ref[...] = h
        ab_ref[...] = jnp.dot(h, wab_ref[...], preferred_element_type=F32)

    o_ref[...] = jnp.dot(h_ref[...], w_ref[...], preferred_element_type=F32)


def _inproj(x2d, mod, gain, w_main, w_ab, *, seq):
    t, d = x2d.shape
    n = w_main.shape[1]
    tm = _pick(seq, (1024, 512, 256, 128))
    tn = _pick(n, (1024, 512, 256, 128))
    return pl.pallas_call(
        _inproj_kernel,
        grid=(t // tm, n // tn),
        in_specs=[pl.BlockSpec((tm, d), lambda i, j: (i, 0)),
                  pl.BlockSpec((1, N_MOD, d), lambda i, j: ((i * tm) // seq, 0, 0)),
                  pl.BlockSpec((1, d), lambda i, j: (0, 0)),
                  pl.BlockSpec((d, tn), lambda i, j: (0, j)),
                  pl.BlockSpec((d, LANES), lambda i, j: (0, 0))],
        out_specs=[pl.BlockSpec((tm, tn), lambda i, j: (i, j)),
                   pl.BlockSpec((tm, LANES), lambda i, j: (i, 0))],
        out_shape=[jax.ShapeDtypeStruct((t, n), F32), jax.ShapeDtypeStruct((t, LANES), F32)],
        scratch_shapes=[pltpu.VMEM((tm, d), BF16)],
        compiler_params=_cparams(("parallel", "arbitrary")),
        name="inproj",
    )(x2d, mod, gain, w_main, w_ab)


def _attn_kernel(pos_ref, invf_ref, sgn_ref, qg_ref, kg_ref, q_ref, k_ref, v_ref, o_ref,
                 cos_ref, sin_ref, qn_ref, kp_ref, vp_ref, ob_ref, lb_ref, *, seq, pad):
    h = pl.program_id(1)
    nb = len(DILATIONS)

    @pl.when(h == 0)
    def _():
        ang = pos_ref[0] * invf_ref[...]
        cos_ref[...] = jnp.cos(ang)
        sin_ref[...] = jnp.sin(ang) * sgn_ref[...]

    lane = lax.broadcasted_iota(jnp.int32, (seq, HEAD_DIM), 1)

    def norm_rope(x, gain):
        y = x * lax.rsqrt(jnp.mean(x * x, axis=-1, keepdims=True) + NORM_EPS) * gain
        partner = jnp.where(lane < ROPE_HALF, pltpu.roll(y, HEAD_DIM - ROPE_HALF, 1),
                            pltpu.roll(y, ROPE_HALF, 1))
        return y * cos_ref[...] + partner * sin_ref[...]

    qn_ref[...] = norm_rope(q_ref[0], qg_ref[...])
    kp_ref[0:pad, :] = jnp.zeros((pad, HEAD_DIM), F32)
    vp_ref[0:pad, :] = jnp.zeros((pad, HEAD_DIM), F32)
    kp_ref[pad:pad + seq, :] = norm_rope(k_ref[0], kg_ref[...])
    vp_ref[pad:pad + seq, :] = v_ref[0]

    qi = lax.broadcasted_iota(jnp.int32, (Q_BLOCK, 2 * Q_BLOCK), 0)
    kj = lax.broadcasted_iota(jnp.int32, (Q_BLOCK, 2 * Q_BLOCK), 1)
    dist = qi + Q_BLOCK - kj
    band = (dist >= 0) & (dist <= Q_BLOCK)
    scale = HEAD_DIM ** -0.5

    for p, d in enumerate(DILATIONS):
        nblk = seq // (d * Q_BLOCK)

        def rows(start, size, d=d):
            return pl.ds(start, size) if d == 1 else pl.ds(start, size, stride=d)

        def residue(r, carry, d=d, nblk=nblk, p=p, rows=rows):
            def block(n, carry2):
                qstart = r + d * Q_BLOCK * n
                kstart = pad + qstart - d * Q_BLOCK
                q = qn_ref[rows(qstart, Q_BLOCK), :].astype(BF16)
                k = kp_ref[rows(kstart, 2 * Q_BLOCK), :].astype(BF16)
                v = vp_ref[rows(kstart, 2 * Q_BLOCK), :].astype(BF16)
                s = lax.dot_general(q, k, (((1,), (1,)), ((), ())), preferred_element_type=F32) * scale
                valid = band & ((kj >= Q_BLOCK) | (n > 0))
                s = jnp.where(valid, s, NEG_INF)
                m = jnp.max(s, axis=-1, keepdims=True)
                e = jnp.exp(s - m)
                l = jnp.sum(e, axis=-1, keepdims=True)
                acc = jnp.dot(e.astype(BF16), v, preferred_element_type=F32)
                ob_ref[rows(p * seq + qstart, Q_BLOCK), :] = acc / l
                lb_ref[rows(p * seq + qstart, Q_BLOCK), :] = jnp.broadcast_to(m + jnp.log(l), (Q_BLOCK, HEAD_DIM))
                return carry2
            return lax.fori_loop(0, nblk, block, carry)

        lax.fori_loop(0, d, residue, 0)

    lses = [lb_ref[p * seq:(p + 1) * seq, :] for p in range(nb)]
    mx = functools.reduce(jnp.maximum, lses)
    ws = [jnp.exp(l - mx) for l in lses]
    num = functools.reduce(lambda a, b: a + b, [w * ob_ref[p * seq:(p + 1) * seq, :] for p, w in enumerate(ws)])
    den = functools.reduce(lambda a, b: a + b, ws)
    o_ref[0] = (num / den).astype(o_ref.dtype)


def _attention(proj, pos_b, invf, sgn, qg, kg):
    b, seq, _ = proj.shape
    pad = DILATIONS[-1] * Q_BLOCK
    ha = N_ATTN_HEADS
    blk = lambda off: pl.BlockSpec((1, seq, HEAD_DIM), lambda i, h: (i, 0, off + h))
    vec = pl.BlockSpec((1, HEAD_DIM), lambda i, h: (0, 0))
    return pl.pallas_call(
        functools.partial(_attn_kernel, seq=seq, pad=pad),
        grid=(b, ha),
        in_specs=[pl.BlockSpec((1, seq, HEAD_DIM), lambda i, h: (i, 0, 0)),
                  vec, vec, vec, vec, blk(0), blk(ha), blk(2 * ha)],
        out_specs=pl.BlockSpec((1, seq, HEAD_DIM), lambda i, h: (i, 0, h)),
        out_shape=jax.ShapeDtypeStruct((b, seq, ATTN_WIDTH), BF16),
        scratch_shapes=[pltpu.VMEM((seq, HEAD_DIM), F32),
                        pltpu.VMEM((seq, HEAD_DIM), F32),
                        pltpu.VMEM((seq, HEAD_DIM), F32),
                        pltpu.VMEM((pad + seq, HEAD_DIM), F32),
                        pltpu.VMEM((pad + seq, HEAD_DIM), F32),
                        pltpu.VMEM((len(DILATIONS) * seq, HEAD_DIM), F32),
                        pltpu.VMEM((len(DILATIONS) * seq, HEAD_DIM), F32)],
        compiler_params=_cparams(("parallel", "arbitrary")),
        name="dilated_attn",
    )(pos_b, invf, sgn, qg, kg, proj, proj, proj)


def _softplus(x):
    return jnp.maximum(x, 0.0) + jnp.log1p(jnp.exp(-jnp.abs(x)))


def _delta_prep_kernel(ab_ref, alog_ref, dtb_ref, cwq_ref, cwk_ref, cwv_ref,
                       q_ref, k_ref, v_ref, hq_ref, hk_ref, hv_ref,
                       u_ref, w_ref, qd_ref, kt_ref, in_ref, dec_ref,
                       xp_ref, qc_ref, kc_ref, vc_ref, g_ref, bt_ref, *, ts):
    i = pl.program_id(1)
    h = pl.program_id(2)
    nchunk = ts // CHUNK

    def conv_silu(x_ref, halo_ref, cw_ref):
        halo = halo_ref[0]
        xp_ref[0:SUBLANES, :] = jnp.where(i > 0, halo, jnp.zeros_like(halo))
        xp_ref[SUBLANES:SUBLANES + ts, :] = x_ref[0]
        y = jnp.zeros((ts, HEAD_DIM), F32)
        for j in range(CONV_WIDTH):
            off = SUBLANES - (CONV_WIDTH - 1) + j
            y = y + cw_ref[j:j + 1, :] * xp_ref[off:off + ts, :]
        return y * jax.nn.sigmoid(y)

    def l2n(x):
        return x * lax.rsqrt(jnp.sum(x * x, axis=-1, keepdims=True) + NORM_EPS)

    qc_ref[...] = l2n(conv_silu(q_ref, hq_ref, cwq_ref)) * (HEAD_DIM ** -0.5)
    kc_ref[...] = l2n(conv_silu(k_ref, hk_ref, cwk_ref))
    vc_ref[...] = conv_silu(v_ref, hv_ref, cwv_ref)

    rr = lax.broadcasted_iota(jnp.int32, (LANES, LANES), 0)
    sel_a = (rr == h).astype(F32)
    sel_b = (rr == h + N_DELTA_HEADS).astype(F32)
    ab = ab_ref[0]
    a_rep = jnp.dot(ab, sel_a, preferred_element_type=F32, precision=lax.Precision.HIGHEST)
    b_rep = jnp.dot(ab, sel_b, preferred_element_type=F32, precision=lax.Precision.HIGHEST)
    g_ref[...] = -jnp.exp(alog_ref[0]) * _softplus(a_rep + dtb_ref[0])
    bt_ref[...] = jax.nn.sigmoid(b_rep)

    ci = lax.broadcasted_iota(jnp.int32, (CHUNK, CHUNK), 0)
    cj = lax.broadcasted_iota(jnp.int32, (CHUNK, CHUNK), 1)
    tril = (ci >= cj).astype(F32)
    eye = (ci == cj).astype(F32)
    ones = jnp.ones((CHUNK, CHUNK), F32)
    hi = lax.Precision.HIGHEST
    nt = (((1,), (1,)), ((), ()))
    merge_masks = []
    s = 1
    while s < CHUNK:
        merge_masks.append(((ci // s) % 2 == 1) & ((cj // s) == (ci // s) - 1))
        s *= 2

    def chunk(c, carry):
        r0 = pl.multiple_of(c * CHUNK, CHUNK)
        sl = pl.ds(r0, CHUNK)
        q = qc_ref[sl, :]
        k = kc_ref[sl, :]
        v = vc_ref[sl, :]
        g = g_ref[sl, :]
        beta = bt_ref[sl, :]
        gc = jnp.dot(tril, g, preferred_element_type=F32, precision=hi)
        gcol = gc[:, :CHUNK]
        grow = jnp.dot(ones, gcol * eye, preferred_element_type=F32, precision=hi)
        gamma = jnp.exp(jnp.where(ci >= cj, gcol - grow, NEG_INF))
        kb = k * beta
        kbf = k.astype(BF16)
        kk = lax.dot_general(kb.astype(BF16), kbf, nt, preferred_element_type=F32)
        a = jnp.where(ci > cj, kk * gamma, 0.0)
        qk = lax.dot_general(q.astype(BF16), kbf, nt, preferred_element_type=F32)
        intra = qk * gamma
        t = eye
        for off in merge_masks:
            am = jnp.where(off, a, 0.0).astype(BF16)
            tb = t.astype(BF16)
            t = t - jnp.dot(tb, jnp.dot(am, tb, preferred_element_type=F32).astype(BF16),
                            preferred_element_type=F32)
        eg = jnp.exp(gc)
        rhs = jnp.concatenate([v * beta, kb * eg], axis=1).astype(BF16)
        uw = jnp.dot(t.astype(BF16), rhs, preferred_element_type=F32)
        glast = gc[CHUNK - 1:CHUNK, :]
        u_ref[0, sl, :] = uw[:, :HEAD_DIM]
        w_ref[0, sl, :] = uw[:, HEAD_DIM:].astype(BF16)
        qd_ref[0, sl, :] = (q * eg).astype(BF16)
        kt_ref[0, sl, :] = (k * jnp.exp(glast - gc)).astype(BF16)
        in_ref[0, sl, :] = jnp.concatenate([intra, jnp.zeros_like(intra)], axis=1).astype(BF16)
        dec_ref[0, 0, pl.ds(c, 1), :] = jnp.exp(glast)
        return carry

    lax.fori_loop(0, nchunk, chunk, 0)


def _delta_prep(proj, ab, alog_rep, dtb_rep, conv_w):
    b, seq, _ = proj.shape
    hd = N_DELTA_HEADS
    ts = _pick(seq, (512,))
    q0 = 3 * N_ATTN_HEADS
    tile = lambda off: pl.BlockSpec((1, ts, HEAD_DIM), lambda bi, i, h: (bi, i, off + h))
    halo = lambda off: pl.BlockSpec(
        (1, SUBLANES, HEAD_DIM),
        lambda bi, i, h: (bi, jnp.maximum(i * (ts // SUBLANES) - 1, 0), off + h))
    cw = lambda off: pl.BlockSpec((CONV_WIDTH, HEAD_DIM), lambda bi, i, h: (0, off + h))
    hvec = pl.BlockSpec((1, 1, HEAD_DIM), lambda bi, i, h: (h, 0, 0))
    out_tile = pl.BlockSpec((1, ts, HEAD_DIM), lambda bi, i, h: (bi, i, h))
    big = lambda dt: jax.ShapeDtypeStruct((b, seq, DELTA_WIDTH), dt)
    return pl.pallas_call(
        functools.partial(_delta_prep_kernel, ts=ts),
        grid=(b, seq // ts, hd),
        in_specs=[pl.BlockSpec((1, ts, LANES), lambda bi, i, h: (bi, i, 0)),
                  hvec, hvec, cw(0), cw(hd), cw(2 * hd),
                  tile(q0), tile(q0 + hd), tile(q0 + 2 * hd),
                  halo(q0), halo(q0 + hd), halo(q0 + 2 * hd)],
        out_specs=[out_tile, out_tile, out_tile, out_tile, out_tile,
                   pl.BlockSpec((1, 1, ts // CHUNK, HEAD_DIM), lambda bi, i, h: (bi, h, i, 0))],
        out_shape=[big(F32), big(BF16), big(BF16), big(BF16), big(BF16),
                   jax.ShapeDtypeStruct((b, hd, seq // CHUNK, HEAD_DIM), F32)],
        scratch_shapes=[pltpu.VMEM((SUBLANES + ts, HEAD_DIM), F32)]
                       + [pltpu.VMEM((ts, HEAD_DIM), F32)] * 5,
        compiler_params=_cparams(("parallel", "parallel", "arbitrary")),
        name="delta_prep",
    )(ab, alog_rep, dtb_rep, conv_w, conv_w, conv_w, proj, proj, proj, proj, proj, proj)


def _delta_scan_kernel(u_ref, w_ref, qd_ref, kt_ref, in_ref, dec_ref, z_ref, gain_ref, o_ref, s_ref,
                       *, nb, cg):
    @pl.when(pl.program_id(0) == 0)
    def _():
        s_ref[...] = jnp.zeros_like(s_ref)

    gain = gain_ref[...]
    tn = (((0,), (0,)), ((), ()))

    def chunk(c, carry):
        sl = pl.ds(pl.multiple_of(c * CHUNK, CHUNK), CHUNK)
        for bi in range(nb):
            for h in range(N_DELTA_HEADS):
                cs = slice(h * HEAD_DIM, (h + 1) * HEAD_DIM)
                st = s_ref[bi * N_DELTA_HEADS + h]
                sb = st.astype(BF16)
                v_new = u_ref[bi, sl, cs] - jnp.dot(w_ref[bi, sl, cs], sb, preferred_element_type=F32)
                vb = v_new.astype(BF16)
                o = (jnp.dot(qd_ref[bi, sl, cs], sb, preferred_element_type=F32)
                     + jnp.dot(in_ref[bi, sl, cs][:, :CHUNK], vb, preferred_element_type=F32))
                dec = dec_ref[bi, h, pl.ds(c, 1), :]
                s_ref[bi * N_DELTA_HEADS + h] = st * dec + lax.dot_general(
                    kt_ref[bi, sl, cs], vb, tn, preferred_element_type=F32)
                y = o * lax.rsqrt(jnp.mean(o * o, axis=-1, keepdims=True) + NORM_EPS) * gain
                z = z_ref[bi, sl, cs]
                o_ref[bi, sl, cs] = (y * (z * jax.nn.sigmoid(z))).astype(o_ref.dtype)
        return carry

    lax.fori_loop(0, cg, chunk, 0)


def _delta_scan(u, w, qd, kt, intra, dec, proj, gain):
    b, seq, _ = u.shape
    cg = SUBLANES
    rows = cg * CHUNK
    zblk = (3 * ATTN_WIDTH + 3 * DELTA_WIDTH) // DELTA_WIDTH
    tile = pl.BlockSpec((b, rows, DELTA_WIDTH), lambda i: (0, i, 0))
    return pl.pallas_call(
        functools.partial(_delta_scan_kernel, nb=b, cg=cg),
        grid=(seq // rows,),
        in_specs=[tile, tile, tile, tile, tile,
                  pl.BlockSpec((b, N_DELTA_HEADS, cg, HEAD_DIM), lambda i: (0, 0, i, 0)),
                  pl.BlockSpec((b, rows, DELTA_WIDTH), lambda i: (0, i, zblk)),
                  pl.BlockSpec((1, HEAD_DIM), lambda i: (0, 0))],
        out_specs=tile,
        out_shape=jax.ShapeDtypeStruct((b, seq, DELTA_WIDTH), BF16),
        scratch_shapes=[pltpu.VMEM((b * N_DELTA_HEADS, HEAD_DIM, HEAD_DIM), F32)],
        compiler_params=_cparams(("arbitrary",)),
        name="delta_scan",
    )(u, w, qd, kt, intra, dec, proj, gain)


def _outproj_kernel(oa_ref, od_ref, wa_ref, wd_ref, x_ref, mod_ref, o_ref):
    y = (jnp.dot(oa_ref[...], wa_ref[...], preferred_element_type=F32)
         + jnp.dot(od_ref[...], wd_ref[...], preferred_element_type=F32))
    o_ref[...] = x_ref[...] + mod_ref[0, 5:6, :] * y


def _outproj(oa, od, wa, wd, x2d, mod, *, seq):
    t, d = x2d.shape
    tm = _pick(seq, (1024, 512, 256, 128))
    tn = _pick(d, (1024, 512, 256, 128))
    return pl.pallas_call(
        _outproj_kernel,
        grid=(t // tm, d // tn),
        in_specs=[pl.BlockSpec((tm, ATTN_WIDTH), lambda i, j: (i, 0)),
                  pl.BlockSpec((tm, DELTA_WIDTH), lambda i, j: (i, 0)),
                  pl.BlockSpec((ATTN_WIDTH, tn), lambda i, j: (0, j)),
                  pl.BlockSpec((DELTA_WIDTH, tn), lambda i, j: (0, j)),
                  pl.BlockSpec((tm, tn), lambda i, j: (i, j)),
                  pl.BlockSpec((1, N_MOD, tn), lambda i, j: ((i * tm) // seq, 0, j))],
        out_specs=pl.BlockSpec((tm, tn), lambda i, j: (i, j)),
        out_shape=jax.ShapeDtypeStruct((t, d), F32),
        compiler_params=_cparams(("parallel", "parallel")),
        name="outproj",
    )(oa, od, wa, wd, x2d, mod)


def _layer(x, mod, positions, ffn1_norm, ffn1_w_gate, ffn1_w_up, ffn1_w_down, mix_norm, w_in, conv_w,
           q_norm, k_norm, a_log, dt_bias, delta_out_norm, w_out, ffn2_norm, ffn2_w_gate, ffn2_w_up,
           ffn2_w_down):
    b, seq, d = x.shape
    t = b * seq
    x2d = x.reshape(t, d)
    row = lambda v: v.reshape(1, -1).astype(F32)

    x2d = _ffn(x2d, mod, row(ffn1_norm), ffn1_w_gate.astype(BF16), ffn1_w_up.astype(BF16),
               ffn1_w_down.astype(BF16), sub=0, seq=seq)

    w_main = w_in[:, :MAIN_PROJ].astype(BF16)
    w_ab = jnp.pad(w_in[:, MAIN_PROJ:], ((0, 0), (0, LANES - (w_in.shape[1] - MAIN_PROJ)))).astype(BF16)
    proj, ab = _inproj(x2d, mod, row(mix_norm), w_main, w_ab, seq=seq)
    proj = proj.reshape(b, seq, MAIN_PROJ)
    ab = ab.reshape(b, seq, LANES)

    lane = jnp.arange(HEAD_DIM)
    inv_freq = ROPE_THETA ** (-(lane % ROPE_HALF).astype(F32) / ROPE_HALF)
    invf = jnp.where(lane < 2 * ROPE_HALF, inv_freq, 0.0).reshape(1, HEAD_DIM).astype(F32)
    sgn = jnp.where(lane < ROPE_HALF, -1.0, jnp.where(lane < 2 * ROPE_HALF, 1.0, 0.0)).reshape(1, HEAD_DIM)
    pos_b = jnp.broadcast_to(positions.astype(F32)[..., None], (b, seq, HEAD_DIM))
    oa = _attention(proj, pos_b, invf, sgn.astype(F32), row(q_norm), row(k_norm))

    rep = lambda v: jnp.broadcast_to(v.astype(F32)[:, None, None], (N_DELTA_HEADS, 1, HEAD_DIM))
    u, w, qd, kt, intra, dec = _delta_prep(proj, ab, rep(a_log), rep(dt_bias), conv_w.astype(F32))
    od = _delta_scan(u, w, qd, kt, intra, dec, proj, row(delta_out_norm))

    w_out_b = w_out.astype(BF16)
    x2d = _outproj(oa.reshape(t, ATTN_WIDTH), od.reshape(t, DELTA_WIDTH),
                   w_out_b[:ATTN_WIDTH], w_out_b[ATTN_WIDTH:], x2d, mod, seq=seq)

    x2d = _ffn(x2d, mod, row(ffn2_norm), ffn2_w_gate.astype(BF16), ffn2_w_up.astype(BF16),
               ffn2_w_down.astype(BF16), sub=2, seq=seq)
    return x2d.reshape(b, seq, d)


def kernel(x, c, positions, w_ada, b_ada, ffn1_norm, ffn1_w_gate, ffn1_w_up, ffn1_w_down, mix_norm, w_in,
           conv_w, q_norm, k_norm, a_log, dt_bias, delta_out_norm, w_out, ffn2_norm, ffn2_w_gate, ffn2_w_up,
           ffn2_w_down):
    b, _, d = x.shape
    depth = w_ada.shape[0]
    c8 = jnp.pad(c.astype(F32), ((0, SUBLANES - b), (0, 0)))
    for l in range(depth):
        mod = _adaln(c8, w_ada[l], b_ada[l].reshape(1, -1))[:b].reshape(b, N_MOD, d)
        x = _layer(x, mod, positions, ffn1_norm[l], ffn1_w_gate[l], ffn1_w_up[l], ffn1_w_down[l],
                   mix_norm[l], w_in[l], conv_w[l], q_norm[l], k_norm[l], a_log[l], dt_bias[l],
                   delta_out_norm[l], w_out[l], ffn2_norm[l], ffn2_w_gate[l], ffn2_w_up[l], ffn2_w_down[l])
    return x
```

```python
import functools

import jax
import jax.numpy as jnp
from jax import lax
from jax.experimental import pallas as pl
from jax.experimental.pallas import tpu as pltpu

F32 = jnp.float32
BF16 = jnp.bfloat16

HEAD_DIM = 128
N_ATTN_HEADS = 8
N_DELTA_HEADS = 8
ATTN_WIDTH = N_ATTN_HEADS * HEAD_DIM
DELTA_WIDTH = N_DELTA_HEADS * HEAD_DIM
MIX_WIDTH = ATTN_WIDTH + DELTA_WIDTH
DILATIONS = (1, 4, 16)
Q_BLOCK = 128
ATTN_UNROLL = 4
ROPE_THETA = 500000.0
ROPE_HALF = HEAD_DIM // 8
CONV_WIDTH = 4
CHUNK = 64
PAIR = 2 * CHUNK
NORM_EPS = 1e-6
N_MOD = 9
MAIN_PROJ = 3 * ATTN_WIDTH + 4 * DELTA_WIDTH
LANES = 128
SUBLANES = 8
VMEM_LIMIT = 56 * 1024 * 1024
NEG_INF = float("-inf")
NT_DIMS = (((1,), (1,)), ((), ()))
TN_DIMS = (((0,), (0,)), ((), ()))

assert PAIR == LANES


def _cparams(semantics):
    return pltpu.CompilerParams(dimension_semantics=semantics, vmem_limit_bytes=VMEM_LIMIT)


def _pick(n, candidates):
    for c in candidates:
        if n % c == 0:
            return c
    return n


def _split_bf16(x, terms):
    parts = []
    for _ in range(terms):
        p = x.astype(BF16)
        parts.append(p)
        x = x - p.astype(F32)
    return parts


def _adaln_kernel(c_ref, w_ref, b_ref, o_ref):
    c = c_ref[...]
    ca = (c * jax.nn.sigmoid(c)).astype(BF16)
    o_ref[...] = jnp.dot(ca, w_ref[...].astype(BF16), preferred_element_type=F32) + b_ref[...]


def _adaln(c8, w, b):
    d, n = w.shape
    tn = _pick(n, (1024, 512, 256, 128))
    return pl.pallas_call(
        _adaln_kernel,
        grid=(n // tn,),
        in_specs=[pl.BlockSpec((SUBLANES, d), lambda j: (0, 0)),
                  pl.BlockSpec((d, tn), lambda j: (0, j)),
                  pl.BlockSpec((1, tn), lambda j: (0, j))],
        out_specs=pl.BlockSpec((SUBLANES, tn), lambda j: (0, j)),
        out_shape=jax.ShapeDtypeStruct((SUBLANES, n), F32),
        compiler_params=_cparams(("parallel",)),
        name="adaln",
    )(c8, w, b)


def _modulated(x, gain, mod_ref, sub):
    ms = jnp.mean(x * x, axis=-1, keepdims=True)
    y = x * lax.rsqrt(ms + NORM_EPS) * gain
    return y * (1.0 + mod_ref[0, 3 * sub + 1:3 * sub + 2, :]) + mod_ref[0, 3 * sub:3 * sub + 1, :]


def _ffn_kernel(x_ref, mod_ref, gain_ref, wg_ref, wu_ref, wd_ref, o_ref, h_ref, *, sub, nf):
    f = pl.program_id(1)

    @pl.when(f == 0)
    def _():
        h_ref[...] = _modulated(x_ref[...], gain_ref[...], mod_ref, sub).astype(BF16)
        o_ref[...] = jnp.zeros_like(o_ref)

    h = h_ref[...]
    g = jnp.dot(h, wg_ref[...], preferred_element_type=F32)
    u = jnp.dot(h, wu_ref[...], preferred_element_type=F32)
    a = (g * jax.nn.sigmoid(g) * u).astype(BF16)
    o_ref[...] += jnp.dot(a, wd_ref[...], preferred_element_type=F32)

    @pl.when(f == nf - 1)
    def _():
        gate = 0.5 * mod_ref[0, 3 * sub + 2:3 * sub + 3, :]
        o_ref[...] = x_ref[...] + gate * o_ref[...]


def _ffn(x2d, mod, gain, wg, wu, wd, *, sub, seq):
    t, d = x2d.shape
    dff = wg.shape[1]
    tm = _pick(seq, (512, 256, 128))
    tf = _pick(dff, (512, 256, 128))
    nf = dff // tf
    return pl.pallas_call(
        functools.partial(_ffn_kernel, sub=sub, nf=nf),
        grid=(t // tm, nf),
        in_specs=[pl.BlockSpec((tm, d), lambda i, f: (i, 0)),
                  pl.BlockSpec((1, N_MOD, d), lambda i, f: ((i * tm) // seq, 0, 0)),
                  pl.BlockSpec((1, d), lambda i, f: (0, 0)),
                  pl.BlockSpec((d, tf), lambda i, f: (0, f)),
                  pl.BlockSpec((d, tf), lambda i, f: (0, f)),
                  pl.BlockSpec((tf, d), lambda i, f: (f, 0))],
        out_specs=pl.BlockSpec((tm, d), lambda i, f: (i, 0)),
        out_shape=jax.ShapeDtypeStruct((t, d), F32),
        scratch_shapes=[pltpu.VMEM((tm, d), BF16)],
        compiler_params=_cparams(("parallel", "arbitrary")),
        name=f"ffn{sub}",
    )(x2d, mod, gain, wg, wu, wd)


def _inproj_kernel(x_ref, mod_ref, gain_ref, w_ref, wab_ref, o_ref, ab_ref, h_ref):
    n = pl.program_id(1)

    @pl.when(n == 0)
    def _():
        h = _modulated(x_ref[...], gain_ref[...], mod_ref, 1).astype(BF16)
        h_ref[...] = h
        ab_ref[...] = jnp.dot(h, wab_ref[...], preferred_element_type=F32)

    o_ref[...] = jnp.dot(h_ref[...], w_ref[...], preferred_element_type=F32)


def _inproj(x2d, mod, gain, w_main, w_ab, *, seq):
    t, d = x2d.shape
    n = w_main.shape[1]
    tm = _pick(seq, (1024, 512, 256, 128))
    tn = _pick(n, (1024, 512, 256, 128))
    return pl.pallas_call(
        _inproj_kernel,
        grid=(t // tm, n // tn),
        in_specs=[pl.BlockSpec((tm, d), lambda i, j: (i, 0)),
                  pl.BlockSpec((1, N_MOD, d), lambda i, j: ((i * tm) // seq, 0, 0)),
                  pl.BlockSpec((1, d), lambda i, j: (0, 0)),
                  pl.BlockSpec((d, tn), lambda i, j: (0, j)),
                  pl.BlockSpec((d, LANES), lambda i, j: (0, 0))],
        out_specs=[pl.BlockSpec((tm, tn), lambda i, j: (i, j)),
                   pl.BlockSpec((tm, LANES), lambda i, j: (i, 0))],
        out_shape=[jax.ShapeDtypeStruct((t, n), F32), jax.ShapeDtypeStruct((t, LANES), F32)],
        scratch_shapes=[pltpu.VMEM((tm, d), BF16)],
        compiler_params=_cparams(("parallel", "arbitrary")),
        name="inproj",
    )(x2d, mod, gain, w_main, w_ab)


def _attn_kernel(pos_ref, invf_ref, sgn_ref, qg_ref, kg_ref, q_ref, k_ref, v_ref, o_ref,
                 cos_ref, sin_ref, qn_ref, kp_ref, vp_ref, ob_ref, lb_ref, bias_ref, *, seq, pad):
    h = pl.program_id(1)
    nb = len(DILATIONS)

    @pl.when(h == 0)
    def _():
        ang = pos_ref[0] * invf_ref[...]
        cos_ref[...] = jnp.cos(ang)
        sin_ref[...] = jnp.sin(ang) * sgn_ref[...]

    lane = lax.broadcasted_iota(jnp.int32, (seq, HEAD_DIM), 1)

    def norm_rope(x, gain):
        y = x * lax.rsqrt(jnp.mean(x * x, axis=-1, keepdims=True) + NORM_EPS) * gain
        partner = jnp.where(lane < ROPE_HALF, pltpu.roll(y, HEAD_DIM - ROPE_HALF, 1),
                            pltpu.roll(y, ROPE_HALF, 1))
        return y * cos_ref[...] + partner * sin_ref[...]

    qn_ref[...] = norm_rope(q_ref[0], qg_ref[...]) * (HEAD_DIM ** -0.5)
    kp_ref[0:pad, :] = jnp.zeros((pad, HEAD_DIM), F32)
    vp_ref[0:pad, :] = jnp.zeros((pad, HEAD_DIM), F32)
    kp_ref[pad:pad + seq, :] = norm_rope(k_ref[0], kg_ref[...])
    vp_ref[pad:pad + seq, :] = v_ref[0]

    qi = lax.broadcasted_iota(jnp.int32, (Q_BLOCK, 2 * Q_BLOCK), 0)
    kj = lax.broadcasted_iota(jnp.int32, (Q_BLOCK, 2 * Q_BLOCK), 1)
    dist = qi + Q_BLOCK - kj
    band = (dist >= 0) & (dist <= Q_BLOCK)
    bias_ref[0] = jnp.where(band, 0.0, NEG_INF)
    bias_ref[1] = jnp.where(band & (kj >= Q_BLOCK), 0.0, NEG_INF)

    nblocks = seq // Q_BLOCK
    for p, d in enumerate(DILATIONS):
        nblk = nblocks // d

        def rows(start, size, d=d):
            return pl.ds(start, size) if d == 1 else pl.ds(start, size, stride=d)

        def group(it, carry, d=d, nblk=nblk, p=p, rows=rows):
            qstart, first = [], []
            for j in range(ATTN_UNROLL):
                t = it * ATTN_UNROLL + j
                r = lax.div(t, nblk)
                n = lax.rem(t, nblk)
                qstart.append(r + d * Q_BLOCK * n)
                first.append(jnp.where(n == 0, 1, 0))
            q = [qn_ref[rows(qs, Q_BLOCK), :].astype(BF16) for qs in qstart]
            k = [kp_ref[rows(pad + qs - d * Q_BLOCK, 2 * Q_BLOCK), :].astype(BF16) for qs in qstart]
            v = [vp_ref[rows(pad + qs - d * Q_BLOCK, 2 * Q_BLOCK), :].astype(BF16) for qs in qstart]
            s = [lax.dot_general(q[j], k[j], NT_DIMS, preferred_element_type=F32) + bias_ref[first[j]]
                 for j in range(ATTN_UNROLL)]
            m = [jnp.max(x, axis=-1, keepdims=True) for x in s]
            e = [jnp.exp(s[j] - m[j]) for j in range(ATTN_UNROLL)]
            l = [jnp.sum(x, axis=-1, keepdims=True) for x in e]
            acc = [jnp.dot(e[j].astype(BF16), v[j], preferred_element_type=F32) for j in range(ATTN_UNROLL)]
            for j in range(ATTN_UNROLL):
                ob_ref[rows(p * seq + qstart[j], Q_BLOCK), :] = acc[j] / l[j]
                lb_ref[rows(p * seq + qstart[j], Q_BLOCK), :] = jnp.broadcast_to(
                    m[j] + jnp.log(l[j]), (Q_BLOCK, HEAD_DIM))
            return carry

        lax.fori_loop(0, nblocks // ATTN_UNROLL, group, 0)

    lses = [lb_ref[p * seq:(p + 1) * seq, :] for p in range(nb)]
    mx = functools.reduce(jnp.maximum, lses)
    ws = [jnp.exp(l - mx) for l in lses]
    num = functools.reduce(lambda a, b: a + b, [w * ob_ref[p * seq:(p + 1) * seq, :] for p, w in enumerate(ws)])
    den = functools.reduce(lambda a, b: a + b, ws)
    o_ref[0] = (num / den).astype(o_ref.dtype)


def _attention(proj, pos_b, invf, sgn, qg, kg):
    b, seq, _ = proj.shape
    pad = DILATIONS[-1] * Q_BLOCK
    ha = N_ATTN_HEADS
    blk = lambda off: pl.BlockSpec((1, seq, HEAD_DIM), lambda i, h: (i, 0, off + h))
    vec = pl.BlockSpec((1, HEAD_DIM), lambda i, h: (0, 0))
    return pl.pallas_call(
        functools.partial(_attn_kernel, seq=seq, pad=pad),
        grid=(b, ha),
        in_specs=[pl.BlockSpec((1, seq, HEAD_DIM), lambda i, h: (i, 0, 0)),
                  vec, vec, vec, vec, blk(0), blk(ha), blk(2 * ha)],
        out_specs=pl.BlockSpec((1, seq, HEAD_DIM), lambda i, h: (i, 0, h)),
        out_shape=jax.ShapeDtypeStruct((b, seq, ATTN_WIDTH), BF16),
        scratch_shapes=[pltpu.VMEM((seq, HEAD_DIM), F32),
                        pltpu.VMEM((seq, HEAD_DIM), F32),
                        pltpu.VMEM((seq, HEAD_DIM), F32),
                        pltpu.VMEM((pad + seq, HEAD_DIM), F32),
                        pltpu.VMEM((pad + seq, HEAD_DIM), F32),
                        pltpu.VMEM((len(DILATIONS) * seq, HEAD_DIM), F32),
                        pltpu.VMEM((len(DILATIONS) * seq, HEAD_DIM), F32),
                        pltpu.VMEM((2, Q_BLOCK, 2 * Q_BLOCK), F32)],
        compiler_params=_cparams(("parallel", "arbitrary")),
        name="dilated_attn",
    )(pos_b, invf, sgn, qg, kg, proj, proj, proj)


def _softplus(x):
    return jnp.maximum(x, 0.0) + jnp.log1p(jnp.exp(-jnp.abs(x)))


def _delta_prep_kernel(ab_ref, alog_ref, dtb_ref, sel_ref, selr_ref, cwq_ref, cwk_ref, cwv_ref,
                       q_ref, k_ref, v_ref, hq_ref, hk_ref, hv_ref,
                       u_ref, w_ref, qd_ref, kt_ref, in_ref, dec_ref,
                       xp_ref, comp_ref, *, ts):
    i = pl.program_id(1)
    h = pl.program_id(2)
    npair = ts // PAIR
    ci = lax.broadcasted_iota(jnp.int32, (PAIR, PAIR), 0)
    cj = lax.broadcasted_iota(jnp.int32, (PAIR, PAIR), 1)
    same = (ci // CHUNK) == (cj // CHUNK)
    causal = same & (ci >= cj)
    strict = same & (ci > cj)
    eye = (ci == cj).astype(F32)

    @pl.when(h == 0)
    def _():
        lane = lax.broadcasted_iota(jnp.int32, (PAIR, LANES), 1)
        tril = causal.astype(F32)
        for p in range(npair):
            rs = slice(p * PAIR, (p + 1) * PAIR)
            ab = ab_ref[0, rs, :]
            g = -jnp.exp(alog_ref[...]) * _softplus(ab + dtb_ref[...])
            gc = jnp.dot(tril, g, preferred_element_type=F32, precision=lax.Precision.HIGHEST)
            comp = jnp.where(lane < N_DELTA_HEADS, gc, jax.nn.sigmoid(ab))
            for t, part in enumerate(_split_bf16(comp, 3)):
                comp_ref[rs, t * LANES:(t + 1) * LANES] = part

    def conv_silu(x_ref, halo_ref, cw_ref):
        halo = halo_ref[0]
        xp_ref[0:SUBLANES, :] = jnp.where(i > 0, halo, jnp.zeros_like(halo))
        xp_ref[SUBLANES:SUBLANES + ts, :] = x_ref[0]
        y = jnp.zeros((ts, HEAD_DIM), F32)
        for j in range(CONV_WIDTH):
            off = SUBLANES - (CONV_WIDTH - 1) + j
            y = y + cw_ref[j:j + 1, :] * xp_ref[off:off + ts, :]
        return y * jax.nn.sigmoid(y)

    ones2 = jnp.ones((2 * HEAD_DIM, HEAD_DIM), BF16)

    def l2n(x):
        ss = jnp.dot(jnp.concatenate(_split_bf16(x * x, 2), axis=1), ones2, preferred_element_type=F32)
        return x * lax.rsqrt(ss + NORM_EPS)

    q = l2n(conv_silu(q_ref, hq_ref, cwq_ref)) * (HEAD_DIM ** -0.5)
    k = l2n(conv_silu(k_ref, hk_ref, cwk_ref))
    v = conv_silu(v_ref, hv_ref, cwv_ref)
    rep = jnp.dot(comp_ref[...], sel_ref[0], preferred_element_type=F32)

    merge_masks = []
    s = 1
    while s < CHUNK:
        merge_masks.append(((ci // s) % 2 == 1) & ((cj // s) == (ci // s) - 1))
        s *= 2
    row = lax.broadcasted_iota(jnp.int32, (PAIR, LANES), 0)

    pairs = range(npair)
    rsl = [slice(p * PAIR, (p + 1) * PAIR) for p in pairs]
    gc = [rep[rs, :LANES] for rs in rsl]
    beta = [rep[rs, LANES:] for rs in rsl]
    grow = [lax.dot_general(selr_ref[0], comp_ref[rs, :], NT_DIMS, preferred_element_type=F32)
            for rs in rsl]
    gamma = [jnp.exp(jnp.where(causal, gc[p] - grow[p], NEG_INF)) for p in pairs]
    kb = [k[rsl[p]] * beta[p] for p in pairs]
    kbf = [k[rs].astype(BF16) for rs in rsl]
    kk = [lax.dot_general(kb[p].astype(BF16), kbf[p], NT_DIMS, preferred_element_type=F32) for p in pairs]
    a = [jnp.where(strict, kk[p] * gamma[p], 0.0) for p in pairs]
    t = [eye - jnp.where(merge_masks[0], a[p], 0.0) for p in pairs]
    for off in merge_masks[1:]:
        tb = [t[p].astype(BF16) for p in pairs]
        x = [jnp.dot(jnp.where(off, a[p], 0.0).astype(BF16), tb[p], preferred_element_type=F32).astype(BF16)
             for p in pairs]
        t = [t[p] - jnp.dot(tb[p], x[p], preferred_element_type=F32) for p in pairs]
    eg = [jnp.exp(gc[p]) for p in pairs]
    uw = [jnp.dot(t[p].astype(BF16),
                  jnp.concatenate([v[rsl[p]] * beta[p], kb[p] * eg[p]], axis=1).astype(BF16),
                  preferred_element_type=F32) for p in pairs]
    for p in pairs:
        rs = rsl[p]
        intra = lax.dot_general(q[rs].astype(BF16), kbf[p], NT_DIMS, preferred_element_type=F32) * gamma[p]
        g0 = gc[p][CHUNK - 1:CHUNK, :]
        g1 = gc[p][PAIR - 1:PAIR, :]
        glast = jnp.where(row < CHUNK, g0, g1)
        u_ref[0, rs, :] = uw[p][:, :HEAD_DIM]
        w_ref[0, rs, :] = uw[p][:, HEAD_DIM:].astype(BF16)
        qd_ref[0, rs, :] = (q[rs] * eg[p]).astype(BF16)
        kt_ref[0, rs, :] = (k[rs] * jnp.exp(glast - gc[p])).astype(BF16)
        in_ref[0, rs, :] = intra.astype(BF16)
        dec_ref[0, 0, 2 * p:2 * p + 1, :] = jnp.exp(g0)
        dec_ref[0, 0, 2 * p + 1:2 * p + 2, :] = jnp.exp(g1)


def _delta_prep(proj, ab, alog_vec, dtb_vec, sel, selr, conv_w):
    b, seq, _ = proj.shape
    hd = N_DELTA_HEADS
    ts = _pick(seq, (1024, 512, 256, 128))
    q0 = 3 * N_ATTN_HEADS
    tile = lambda off: pl.BlockSpec((1, ts, HEAD_DIM), lambda bi, i, h: (bi, i, off + h))
    halo = lambda off: pl.BlockSpec(
        (1, SUBLANES, HEAD_DIM),
        lambda bi, i, h: (bi, jnp.maximum(i * (ts // SUBLANES) - 1, 0), off + h))
    cw = lambda off: pl.BlockSpec((CONV_WIDTH, HEAD_DIM), lambda bi, i, h: (0, off + h))
    vec = pl.BlockSpec((1, LANES), lambda bi, i, h: (0, 0))
    out_tile = pl.BlockSpec((1, ts, HEAD_DIM), lambda bi, i, h: (bi, i, h))
    big = lambda dt: jax.ShapeDtypeStruct((b, seq, DELTA_WIDTH), dt)
    return pl.pallas_call(
        functools.partial(_delta_prep_kernel, ts=ts),
        grid=(b, seq // ts, hd),
        in_specs=[pl.BlockSpec((1, ts, LANES), lambda bi, i, h: (bi, i, 0)),
                  vec, vec,
                  pl.BlockSpec((1, 3 * LANES, 2 * LANES), lambda bi, i, h: (h, 0, 0)),
                  pl.BlockSpec((1, PAIR, 3 * LANES), lambda bi, i, h: (h, 0, 0)),
                  cw(0), cw(hd), cw(2 * hd),
                  tile(q0), tile(q0 + hd), tile(q0 + 2 * hd),
                  halo(q0), halo(q0 + hd), halo(q0 + 2 * hd)],
        out_specs=[out_tile, out_tile, out_tile, out_tile, out_tile,
                   pl.BlockSpec((1, 1, ts // CHUNK, HEAD_DIM), lambda bi, i, h: (bi, h, i, 0))],
        out_shape=[big(F32), big(BF16), big(BF16), big(BF16), big(BF16),
                   jax.ShapeDtypeStruct((b, hd, seq // CHUNK, HEAD_DIM), F32)],
        scratch_shapes=[pltpu.VMEM((SUBLANES + ts, HEAD_DIM), F32),
                        pltpu.VMEM((ts, 3 * LANES), BF16)],
        compiler_params=_cparams(("parallel", "parallel", "arbitrary")),
        name="delta_prep",
    )(ab, alog_vec, dtb_vec, sel, selr, conv_w, conv_w, conv_w, proj, proj, proj, proj, proj, proj)


def _selectors():
    hd = N_DELTA_HEADS
    src = jnp.arange(3 * LANES) % LANES
    dst = jnp.arange(2 * LANES) // LANES
    head = jnp.arange(hd)[:, None, None]
    sel = (src[None, :, None] == head + hd * dst[None, None, :]).astype(BF16)
    selr = jnp.broadcast_to((src[None, None, :] == head).astype(BF16), (hd, PAIR, 3 * LANES))
    return sel, selr


def _delta_scan_kernel(u_ref, w_ref, qd_ref, kt_ref, in_ref, dec_ref, z_ref, gain_ref, o_ref, s_ref,
                       *, nb, npair):
    @pl.when(pl.program_id(0) == 0)
    def _():
        s_ref[...] = jnp.zeros_like(s_ref)

    gain = gain_ref[...]

    chains = [(bi, h) for bi in range(nb) for h in range(N_DELTA_HEADS)]
    cols = lambda h: slice(h * HEAD_DIM, (h + 1) * HEAD_DIM)

    def pair(pi, carry):
        for e in range(2):
            sl = pl.ds(pl.multiple_of(pi * PAIR + e * CHUNK, CHUNK), CHUNK)
            st = [s_ref[bi * N_DELTA_HEADS + h] for bi, h in chains]
            sb = [s.astype(BF16) for s in st]
            ws = [jnp.dot(w_ref[bi, sl, cols(h)], sb[n], preferred_element_type=F32)
                  for n, (bi, h) in enumerate(chains)]
            qs = [jnp.dot(qd_ref[bi, sl, cols(h)], sb[n], preferred_element_type=F32)
                  for n, (bi, h) in enumerate(chains)]
            vb = [(u_ref[bi, sl, cols(h)] - ws[n]).astype(BF16) for n, (bi, h) in enumerate(chains)]
            zero = jnp.zeros((CHUNK, HEAD_DIM), BF16)
            vpair = [jnp.concatenate([x, zero] if e == 0 else [zero, x], axis=0) for x in vb]
            o = [qs[n] + jnp.dot(in_ref[bi, sl, cols(h)], vpair[n], preferred_element_type=F32)
                 for n, (bi, h) in enumerate(chains)]
            for n, (bi, h) in enumerate(chains):
                dec = dec_ref[bi, h, pl.ds(2 * pi + e, 1), :]
                s_ref[bi * N_DELTA_HEADS + h] = st[n] * dec + lax.dot_general(
                    kt_ref[bi, sl, cols(h)], vb[n], TN_DIMS, preferred_element_type=F32)
            for n, (bi, h) in enumerate(chains):
                y = o[n] * lax.rsqrt(jnp.mean(o[n] * o[n], axis=-1, keepdims=True) + NORM_EPS) * gain
                z = z_ref[bi, sl, cols(h)]
                o_ref[bi, sl, cols(h)] = (y * (z * jax.nn.sigmoid(z))).astype(o_ref.dtype)
        return carry

    lax.fori_loop(0, npair, pair, 0)


def _delta_scan(u, w, qd, kt, intra, dec, proj, gain):
    b, seq, _ = u.shape
    cg = SUBLANES
    rows = cg * CHUNK
    zblk = (3 * ATTN_WIDTH + 3 * DELTA_WIDTH) // DELTA_WIDTH
    tile = pl.BlockSpec((b, rows, DELTA_WIDTH), lambda i: (0, i, 0))
    return pl.pallas_call(
        functools.partial(_delta_scan_kernel, nb=b, npair=rows // PAIR),
        grid=(seq // rows,),
        in_specs=[tile, tile, tile, tile, tile,
                  pl.BlockSpec((b, N_DELTA_HEADS, cg, HEAD_DIM), lambda i: (0, 0, i, 0)),
                  pl.BlockSpec((b, rows, DELTA_WIDTH), lambda i: (0, i, zblk)),
                  pl.BlockSpec((1, HEAD_DIM), lambda i: (0, 0))],
        out_specs=tile,
        out_shape=jax.ShapeDtypeStruct((b, seq, DELTA_WIDTH), BF16),
        scratch_shapes=[pltpu.VMEM((b * N_DELTA_HEADS, HEAD_DIM, HEAD_DIM), F32)],
        compiler_params=_cparams(("arbitrary",)),
        name="delta_scan",
    )(u, w, qd, kt, intra, dec, proj, gain)


def _outproj_kernel(oa_ref, od_ref, wa_ref, wd_ref, x_ref, mod_ref, o_ref):
    y = (jnp.dot(oa_ref[...], wa_ref[...], preferred_element_type=F32)
         + jnp.dot(od_ref[...], wd_ref[...], preferred_element_type=F32))
    o_ref[...] = x_ref[...] + mod_ref[0, 5:6, :] * y


def _outproj(oa, od, wa, wd, x2d, mod, *, seq):
    t, d = x2d.shape
    tm = _pick(seq, (1024, 512, 256, 128))
    tn = _pick(d, (1024, 512, 256, 128))
    return pl.pallas_call(
        _outproj_kernel,
        grid=(t // tm, d // tn),
        in_specs=[pl.BlockSpec((tm, ATTN_WIDTH), lambda i, j: (i, 0)),
                  pl.BlockSpec((tm, DELTA_WIDTH), lambda i, j: (i, 0)),
                  pl.BlockSpec((ATTN_WIDTH, tn), lambda i, j: (0, j)),
                  pl.BlockSpec((DELTA_WIDTH, tn), lambda i, j: (0, j)),
                  pl.BlockSpec((tm, tn), lambda i, j: (i, j)),
                  pl.BlockSpec((1, N_MOD, tn), lambda i, j: ((i * tm) // seq, 0, j))],
        out_specs=pl.BlockSpec((tm, tn), lambda i, j: (i, j)),
        out_shape=jax.ShapeDtypeStruct((t, d), F32),
        compiler_params=_cparams(("parallel", "parallel")),
        name="outproj",
    )(oa, od, wa, wd, x2d, mod)


def _layer(x, mod, positions, ffn1_norm, ffn1_w_gate, ffn1_w_up, ffn1_w_down, mix_norm, w_in, conv_w,
           q_norm, k_norm, a_log, dt_bias, delta_out_norm, w_out, ffn2_norm, ffn2_w_gate, ffn2_w_up,
           ffn2_w_down):
    b, seq, d = x.shape
    t = b * seq
    x2d = x.reshape(t, d)
    row = lambda v: v.reshape(1, -1).astype(F32)

    x2d = _ffn(x2d, mod, row(ffn1_norm), ffn1_w_gate.astype(BF16), ffn1_w_up.astype(BF16),
               ffn1_w_down.astype(BF16), sub=0, seq=seq)

    w_main = w_in[:, :MAIN_PROJ].astype(BF16)
    w_ab = jnp.pad(w_in[:, MAIN_PROJ:], ((0, 0), (0, LANES - (w_in.shape[1] - MAIN_PROJ)))).astype(BF16)
    proj, ab = _inproj(x2d, mod, row(mix_norm), w_main, w_ab, seq=seq)
    proj = proj.reshape(b, seq, MAIN_PROJ)
    ab = ab.reshape(b, seq, LANES)

    lane = jnp.arange(HEAD_DIM)
    inv_freq = ROPE_THETA ** (-(lane % ROPE_HALF).astype(F32) / ROPE_HALF)
    invf = jnp.where(lane < 2 * ROPE_HALF, inv_freq, 0.0).reshape(1, HEAD_DIM).astype(F32)
    sgn = jnp.where(lane < ROPE_HALF, -1.0, jnp.where(lane < 2 * ROPE_HALF, 1.0, 0.0)).reshape(1, HEAD_DIM)
    pos_b = jnp.broadcast_to(positions.astype(F32)[..., None], (b, seq, HEAD_DIM))
    oa = _attention(proj, pos_b, invf, sgn.astype(F32), row(q_norm), row(k_norm))

    lane_vec = lambda v: jnp.pad(v.astype(F32), (0, LANES - N_DELTA_HEADS)).reshape(1, LANES)
    sel, selr = _selectors()
    u, w, qd, kt, intra, dec = _delta_prep(proj, ab, lane_vec(a_log), lane_vec(dt_bias), sel, selr,
                                           conv_w.astype(F32))
    od = _delta_scan(u, w, qd, kt, intra, dec, proj, row(delta_out_norm))

    w_out_b = w_out.astype(BF16)
    x2d = _outproj(oa.reshape(t, ATTN_WIDTH), od.reshape(t, DELTA_WIDTH),
                   w_out_b[:ATTN_WIDTH], w_out_b[ATTN_WIDTH:], x2d, mod, seq=seq)

    x2d = _ffn(x2d, mod, row(ffn2_norm), ffn2_w_gate.astype(BF16), ffn2_w_up.astype(BF16),
               ffn2_w_down.astype(BF16), sub=2, seq=seq)
    return x2d.reshape(b, seq, d)


def kernel(x, c, positions, w_ada, b_ada, ffn1_norm, ffn1_w_gate, ffn1_w_up, ffn1_w_down, mix_norm, w_in,
           conv_w, q_norm, k_norm, a_log, dt_bias, delta_out_norm, w_out, ffn2_norm, ffn2_w_gate, ffn2_w_up,
           ffn2_w_down):
    b, _, d = x.shape
    depth = w_ada.shape[0]
    c8 = jnp.pad(c.astype(F32), ((0, SUBLANES - b), (0, 0)))
    for l in range(depth):
        mod = _adaln(c8, w_ada[l], b_ada[l].reshape(1, -1))[:b].reshape(b, N_MOD, d)
        x = _layer(x, mod, positions, ffn1_norm[l], ffn1_w_gate[l], ffn1_w_up[l], ffn1_w_down[l],
                   mix_norm[l], w_in[l], conv_w[l], q_norm[l], k_norm[l], a_log[l], dt_bias[l],
                   delta_out_norm[l], w_out[l], ffn2_norm[l], ffn2_w_gate[l], ffn2_w_up[l], ffn2_w_down[l])
    return x
```

```python
import functools

import jax
import jax.numpy as jnp
from jax import lax
from jax.experimental import pallas as pl
from jax.experimental.pallas import tpu as pltpu

F32 = jnp.float32
BF16 = jnp.bfloat16

HEAD_DIM = 128
N_ATTN_HEADS = 8
N_DELTA_HEADS = 8
ATTN_WIDTH = N_ATTN_HEADS * HEAD_DIM
DELTA_WIDTH = N_DELTA_HEADS * HEAD_DIM
MIX_WIDTH = ATTN_WIDTH + DELTA_WIDTH
DILATIONS = (1, 4, 16)
Q_BLOCK = 128
ATTN_UNROLL = 4
ROPE_THETA = 500000.0
ROPE_HALF = HEAD_DIM // 8
CONV_WIDTH = 4
CHUNK = 64
PAIR = 2 * CHUNK
NORM_EPS = 1e-6
N_MOD = 9
MAIN_PROJ = 3 * ATTN_WIDTH + 4 * DELTA_WIDTH
LANES = 128
SUBLANES = 8
VMEM_LIMIT = 56 * 1024 * 1024
NEG_INF = float("-inf")
NT_DIMS = (((1,), (1,)), ((), ()))
TN_DIMS = (((0,), (0,)), ((), ()))

assert PAIR == LANES


def _cparams(semantics):
    return pltpu.CompilerParams(dimension_semantics=semantics, vmem_limit_bytes=VMEM_LIMIT)


def _pick(n, candidates):
    for c in candidates:
        if n % c == 0:
            return c
    return n


def _split_bf16(x, terms):
    parts = []
    for _ in range(terms):
        p = x.astype(BF16)
        parts.append(p)
        x = x - p.astype(F32)
    return parts


def _adaln_kernel(c_ref, w_ref, b_ref, o_ref):
    c = c_ref[...]
    ca = (c * jax.nn.sigmoid(c)).astype(BF16)
    o_ref[...] = jnp.dot(ca, w_ref[...].astype(BF16), preferred_element_type=F32) + b_ref[...]


def _adaln(c8, w, b):
    d, n = w.shape
    tn = _pick(n, (1024, 512, 256, 128))
    return pl.pallas_call(
        _adaln_kernel,
        grid=(n // tn,),
        in_specs=[pl.BlockSpec((SUBLANES, d), lambda j: (0, 0)),
                  pl.BlockSpec((d, tn), lambda j: (0, j)),
                  pl.BlockSpec((1, tn), lambda j: (0, j))],
        out_specs=pl.BlockSpec((SUBLANES, tn), lambda j: (0, j)),
        out_shape=jax.ShapeDtypeStruct((SUBLANES, n), F32),
        compiler_params=_cparams(("parallel",)),
        name="adaln",
    )(c8, w, b)


def _modulated(x, gain, mod_ref, sub):
    ms = jnp.mean(x * x, axis=-1, keepdims=True)
    y = x * lax.rsqrt(ms + NORM_EPS) * gain
    return y * (1.0 + mod_ref[0, 3 * sub + 1:3 * sub + 2, :]) + mod_ref[0, 3 * sub:3 * sub + 1, :]


def _ffn_kernel(x_ref, mod_ref, gain_ref, wg_ref, wu_ref, wd_ref, o_ref, h_ref, *, sub, nf, nsplit):
    f = pl.program_id(1)

    @pl.when(f == 0)
    def _():
        h_ref[...] = _modulated(x_ref[...], gain_ref[...], mod_ref, sub).astype(BF16)
        o_ref[...] = jnp.zeros_like(o_ref)

    h = h_ref[...]
    g = jnp.dot(h, wg_ref[...].astype(BF16), preferred_element_type=F32)
    u = jnp.dot(h, wu_ref[...].astype(BF16), preferred_element_type=F32)
    a = (g * jax.nn.sigmoid(g) * u).astype(BF16)
    wd = wd_ref[...].astype(BF16)
    dn = o_ref.shape[1] // nsplit
    for n in range(nsplit):
        cs = slice(n * dn, (n + 1) * dn)
        o_ref[:, cs] += jnp.dot(a, wd[:, cs], preferred_element_type=F32)

    @pl.when(f == nf - 1)
    def _():
        gate = 0.5 * mod_ref[0, 3 * sub + 2:3 * sub + 3, :]
        o_ref[...] = x_ref[...] + gate * o_ref[...]


def _ffn(x2d, mod, gain, wg, wu, wd, layer, *, sub, seq):
    t, d = x2d.shape
    dff = wg.shape[2]
    tm = _pick(seq, (1024, 512, 256, 128))
    tf = _pick(dff, (256, 128))
    nf = dff // tf
    return pl.pallas_call(
        functools.partial(_ffn_kernel, sub=sub, nf=nf, nsplit=d // _pick(d, (512, 256, 128))),
        grid=(t // tm, nf),
        in_specs=[pl.BlockSpec((tm, d), lambda i, f: (i, 0), pipeline_mode=pl.Buffered(1)),
                  pl.BlockSpec((1, N_MOD, d), lambda i, f: ((i * tm) // seq, 0, 0)),
                  pl.BlockSpec((1, d), lambda i, f: (0, 0)),
                  pl.BlockSpec((None, d, tf), lambda i, f: (layer, 0, f)),
                  pl.BlockSpec((None, d, tf), lambda i, f: (layer, 0, f)),
                  pl.BlockSpec((None, tf, d), lambda i, f: (layer, f, 0))],
        out_specs=pl.BlockSpec((tm, d), lambda i, f: (i, 0)),
        out_shape=jax.ShapeDtypeStruct((t, d), F32),
        scratch_shapes=[pltpu.VMEM((tm, d), BF16)],
        compiler_params=_cparams(("parallel", "arbitrary")),
        name=f"ffn{sub}",
    )(x2d, mod, gain, wg, wu, wd)


def _inproj_kernel(x_ref, mod_ref, gain_ref, w_ref, wab_ref, o_ref, ab_ref, h_ref):
    n = pl.program_id(1)

    @pl.when(n == 0)
    def _():
        h = _modulated(x_ref[...], gain_ref[...], mod_ref, 1).astype(BF16)
        h_ref[...] = h
        ab_ref[...] = jnp.dot(h, wab_ref[...], preferred_element_type=F32)

    o_ref[...] = jnp.dot(h_ref[...], w_ref[...].astype(BF16), preferred_element_type=F32)


def _inproj(x2d, mod, gain, w_in, w_ab, layer, *, seq):
    t, d = x2d.shape
    n = MAIN_PROJ
    tm = _pick(seq, (1024, 512, 256, 128))
    tn = _pick(n, (1024, 512, 256, 128))
    return pl.pallas_call(
        _inproj_kernel,
        grid=(t // tm, n // tn),
        in_specs=[pl.BlockSpec((tm, d), lambda i, j: (i, 0)),
                  pl.BlockSpec((1, N_MOD, d), lambda i, j: ((i * tm) // seq, 0, 0)),
                  pl.BlockSpec((1, d), lambda i, j: (0, 0)),
                  pl.BlockSpec((None, d, tn), lambda i, j: (layer, 0, j)),
                  pl.BlockSpec((d, LANES), lambda i, j: (0, 0))],
        out_specs=[pl.BlockSpec((tm, tn), lambda i, j: (i, j)),
                   pl.BlockSpec((tm, LANES), lambda i, j: (i, 0))],
        out_shape=[jax.ShapeDtypeStruct((t, n), F32), jax.ShapeDtypeStruct((t, LANES), F32)],
        scratch_shapes=[pltpu.VMEM((tm, d), BF16)],
        compiler_params=_cparams(("parallel", "arbitrary")),
        name="inproj",
    )(x2d, mod, gain, w_in, w_ab)


def _attn_kernel(pos_ref, invf_ref, sgn_ref, qg_ref, kg_ref, q_ref, k_ref, v_ref, o_ref,
                 cos_ref, sin_ref, qn_ref, kp_ref, vp_ref, ob_ref, lb_ref, bias_ref, *, seq, pad):
    h = pl.program_id(1)
    nb = len(DILATIONS)

    @pl.when(h == 0)
    def _():
        ang = pos_ref[0] * invf_ref[...]
        cos_ref[...] = jnp.cos(ang)
        sin_ref[...] = jnp.sin(ang) * sgn_ref[...]

    lane = lax.broadcasted_iota(jnp.int32, (seq, HEAD_DIM), 1)

    def norm_rope(x, gain):
        y = x * lax.rsqrt(jnp.mean(x * x, axis=-1, keepdims=True) + NORM_EPS) * gain
        partner = jnp.where(lane < ROPE_HALF, pltpu.roll(y, HEAD_DIM - ROPE_HALF, 1),
                            pltpu.roll(y, ROPE_HALF, 1))
        return y * cos_ref[...] + partner * sin_ref[...]

    qn_ref[...] = norm_rope(q_ref[0], qg_ref[...]) * (HEAD_DIM ** -0.5)
    kp_ref[0:pad, :] = jnp.zeros((pad, HEAD_DIM), F32)
    vp_ref[0:pad, :] = jnp.zeros((pad, HEAD_DIM), F32)
    kp_ref[pad:pad + seq, :] = norm_rope(k_ref[0], kg_ref[...])
    vp_ref[pad:pad + seq, :] = v_ref[0]

    qi = lax.broadcasted_iota(jnp.int32, (Q_BLOCK, 2 * Q_BLOCK), 0)
    kj = lax.broadcasted_iota(jnp.int32, (Q_BLOCK, 2 * Q_BLOCK), 1)
    dist = qi + Q_BLOCK - kj
    band = (dist >= 0) & (dist <= Q_BLOCK)
    bias_ref[0] = jnp.where(band, 0.0, NEG_INF)
    bias_ref[1] = jnp.where(band & (kj >= Q_BLOCK), 0.0, NEG_INF)

    nblocks = seq // Q_BLOCK
    for p, d in enumerate(DILATIONS):
        nblk = nblocks // d

        def rows(start, size, d=d):
            return pl.ds(start, size) if d == 1 else pl.ds(start, size, stride=d)

        def group(it, carry, d=d, nblk=nblk, p=p, rows=rows):
            qstart, first = [], []
            for j in range(ATTN_UNROLL):
                t = it * ATTN_UNROLL + j
                r = lax.div(t, nblk)
                n = lax.rem(t, nblk)
                qstart.append(r + d * Q_BLOCK * n)
                first.append(jnp.where(n == 0, 1, 0))
            q = [qn_ref[rows(qs, Q_BLOCK), :].astype(BF16) for qs in qstart]
            k = [kp_ref[rows(pad + qs - d * Q_BLOCK, 2 * Q_BLOCK), :].astype(BF16) for qs in qstart]
            v = [vp_ref[rows(pad + qs - d * Q_BLOCK, 2 * Q_BLOCK), :].astype(BF16) for qs in qstart]
            s = [lax.dot_general(q[j], k[j], NT_DIMS, preferred_element_type=F32) + bias_ref[first[j]]
                 for j in range(ATTN_UNROLL)]
            m = [jnp.max(x, axis=-1, keepdims=True) for x in s]
            e = [jnp.exp(s[j] - m[j]) for j in range(ATTN_UNROLL)]
            l = [jnp.sum(x, axis=-1, keepdims=True) for x in e]
            acc = [jnp.dot(e[j].astype(BF16), v[j], preferred_element_type=F32) for j in range(ATTN_UNROLL)]
            for j in range(ATTN_UNROLL):
                ob_ref[rows(p * seq + qstart[j], Q_BLOCK), :] = acc[j] / l[j]
                lb_ref[rows(p * seq + qstart[j], Q_BLOCK), :] = jnp.broadcast_to(
                    m[j] + jnp.log(l[j]), (Q_BLOCK, HEAD_DIM))
            return carry

        lax.fori_loop(0, nblocks // ATTN_UNROLL, group, 0)

    lses = [lb_ref[p * seq:(p + 1) * seq, :] for p in range(nb)]
    mx = functools.reduce(jnp.maximum, lses)
    ws = [jnp.exp(l - mx) for l in lses]
    num = functools.reduce(lambda a, b: a + b, [w * ob_ref[p * seq:(p + 1) * seq, :] for p, w in enumerate(ws)])
    den = functools.reduce(lambda a, b: a + b, ws)
    o_ref[0] = (num / den).astype(o_ref.dtype)


def _attention(proj, pos_b, invf, sgn, qg, kg):
    b, seq, _ = proj.shape
    pad = DILATIONS[-1] * Q_BLOCK
    ha = N_ATTN_HEADS
    blk = lambda off: pl.BlockSpec((1, seq, HEAD_DIM), lambda i, h: (i, 0, off + h))
    vec = pl.BlockSpec((1, HEAD_DIM), lambda i, h: (0, 0))
    return pl.pallas_call(
        functools.partial(_attn_kernel, seq=seq, pad=pad),
        grid=(b, ha),
        in_specs=[pl.BlockSpec((1, seq, HEAD_DIM), lambda i, h: (i, 0, 0)),
                  vec, vec, vec, vec, blk(0), blk(ha), blk(2 * ha)],
        out_specs=pl.BlockSpec((1, seq, HEAD_DIM), lambda i, h: (i, 0, h)),
        out_shape=jax.ShapeDtypeStruct((b, seq, ATTN_WIDTH), BF16),
        scratch_shapes=[pltpu.VMEM((seq, HEAD_DIM), F32),
                        pltpu.VMEM((seq, HEAD_DIM), F32),
                        pltpu.VMEM((seq, HEAD_DIM), F32),
                        pltpu.VMEM((pad + seq, HEAD_DIM), F32),
                        pltpu.VMEM((pad + seq, HEAD_DIM), F32),
                        pltpu.VMEM((len(DILATIONS) * seq, HEAD_DIM), F32),
                        pltpu.VMEM((len(DILATIONS) * seq, HEAD_DIM), F32),
                        pltpu.VMEM((2, Q_BLOCK, 2 * Q_BLOCK), F32)],
        compiler_params=_cparams(("parallel", "arbitrary")),
        name="dilated_attn",
    )(pos_b, invf, sgn, qg, kg, proj, proj, proj)


def _softplus(x):
    return jnp.maximum(x, 0.0) + jnp.log1p(jnp.exp(-jnp.abs(x)))


def _delta_prep_kernel(ab_ref, alog_ref, dtb_ref, sel_ref, selr_ref, cwq_ref, cwk_ref, cwv_ref,
                       q_ref, k_ref, v_ref, hq_ref, hk_ref, hv_ref,
                       u_ref, w_ref, qd_ref, kt_ref, in_ref, dec_ref,
                       xp_ref, comp_ref, *, ts):
    i = pl.program_id(1)
    h = pl.program_id(2)
    npair = ts // PAIR
    ci = lax.broadcasted_iota(jnp.int32, (PAIR, PAIR), 0)
    cj = lax.broadcasted_iota(jnp.int32, (PAIR, PAIR), 1)
    same = (ci // CHUNK) == (cj // CHUNK)
    causal = same & (ci >= cj)
    strict = same & (ci > cj)
    eye = (ci == cj).astype(F32)

    @pl.when(h == 0)
    def _():
        lane = lax.broadcasted_iota(jnp.int32, (PAIR, LANES), 1)
        tril = causal.astype(F32)
        for p in range(npair):
            rs = slice(p * PAIR, (p + 1) * PAIR)
            ab = ab_ref[0, rs, :]
            g = -jnp.exp(alog_ref[...]) * _softplus(ab + dtb_ref[...])
            gc = jnp.dot(tril, g, preferred_element_type=F32, precision=lax.Precision.HIGHEST)
            comp = jnp.where(lane < N_DELTA_HEADS, gc, jax.nn.sigmoid(ab))
            for t, part in enumerate(_split_bf16(comp, 3)):
                comp_ref[rs, t * LANES:(t + 1) * LANES] = part

    def conv_silu(x_ref, halo_ref, cw_ref):
        halo = halo_ref[0]
        xp_ref[0:SUBLANES, :] = jnp.where(i > 0, halo, jnp.zeros_like(halo))
        xp_ref[SUBLANES:SUBLANES + ts, :] = x_ref[0]
        y = jnp.zeros((ts, HEAD_DIM), F32)
        for j in range(CONV_WIDTH):
            off = SUBLANES - (CONV_WIDTH - 1) + j
            y = y + cw_ref[j:j + 1, :] * xp_ref[off:off + ts, :]
        return y * jax.nn.sigmoid(y)

    ones2 = jnp.ones((2 * HEAD_DIM, HEAD_DIM), BF16)

    def l2n(x):
        ss = jnp.dot(jnp.concatenate(_split_bf16(x * x, 2), axis=1), ones2, preferred_element_type=F32)
        return x * lax.rsqrt(ss + NORM_EPS)

    q = l2n(conv_silu(q_ref, hq_ref, cwq_ref)) * (HEAD_DIM ** -0.5)
    k = l2n(conv_silu(k_ref, hk_ref, cwk_ref))
    v = conv_silu(v_ref, hv_ref, cwv_ref)
    rep = jnp.dot(comp_ref[...], sel_ref[0], preferred_element_type=F32)

    merge_masks = []
    s = 1
    while s < CHUNK:
        merge_masks.append(((ci // s) % 2 == 1) & ((cj // s) == (ci // s) - 1))
        s *= 2
    row = lax.broadcasted_iota(jnp.int32, (PAIR, LANES), 0)

    pairs = range(npair)
    rsl = [slice(p * PAIR, (p + 1) * PAIR) for p in pairs]
    gc = [rep[rs, :LANES] for rs in rsl]
    beta = [rep[rs, LANES:] for rs in rsl]
    grow = [lax.dot_general(selr_ref[0], comp_ref[rs, :], NT_DIMS, preferred_element_type=F32)
            for rs in rsl]
    gamma = [jnp.exp(jnp.where(causal, gc[p] - grow[p], NEG_INF)) for p in pairs]
    kb = [k[rsl[p]] * beta[p] for p in pairs]
    kbf = [k[rs].astype(BF16) for rs in rsl]
    kk = [lax.dot_general(kb[p].astype(BF16), kbf[p], NT_DIMS, preferred_element_type=F32) for p in pairs]
    a = [jnp.where(strict, kk[p] * gamma[p], 0.0) for p in pairs]
    t = [eye - jnp.where(merge_masks[0], a[p], 0.0) for p in pairs]
    for off in merge_masks[1:]:
        tb = [t[p].astype(BF16) for p in pairs]
        x = [jnp.dot(jnp.where(off, a[p], 0.0).astype(BF16), tb[p], preferred_element_type=F32).astype(BF16)
             for p in pairs]
        t = [t[p] - jnp.dot(tb[p], x[p], preferred_element_type=F32) for p in pairs]
    eg = [jnp.exp(gc[p]) for p in pairs]
    uw = [jnp.dot(t[p].astype(BF16),
                  jnp.concatenate([v[rsl[p]] * beta[p], kb[p] * eg[p]], axis=1).astype(BF16),
                  preferred_element_type=F32) for p in pairs]
    for p in pairs:
        rs = rsl[p]
        intra = lax.dot_general(q[rs].astype(BF16), kbf[p], NT_DIMS, preferred_element_type=F32) * gamma[p]
        g0 = gc[p][CHUNK - 1:CHUNK, :]
        g1 = gc[p][PAIR - 1:PAIR, :]
        glast = jnp.where(row < CHUNK, g0, g1)
        u_ref[0, rs, :] = uw[p][:, :HEAD_DIM]
        w_ref[0, rs, :] = uw[p][:, HEAD_DIM:].astype(BF16)
        qd_ref[0, rs, :] = (q[rs] * eg[p]).astype(BF16)
        kt_ref[0, rs, :] = (k[rs] * jnp.exp(glast - gc[p])).astype(BF16)
        in_ref[0, rs, :] = intra.astype(BF16)
        dec_ref[0, 0, 2 * p:2 * p + 1, :] = jnp.exp(g0)
        dec_ref[0, 0, 2 * p + 1:2 * p + 2, :] = jnp.exp(g1)


def _delta_prep(proj, ab, alog_vec, dtb_vec, sel, selr, conv_w):
    b, seq, _ = proj.shape
    hd = N_DELTA_HEADS
    ts = _pick(seq, (1024, 512, 256, 128))
    q0 = 3 * N_ATTN_HEADS
    tile = lambda off: pl.BlockSpec((1, ts, HEAD_DIM), lambda bi, i, h: (bi, i, off + h))
    halo = lambda off: pl.BlockSpec(
        (1, SUBLANES, HEAD_DIM),
        lambda bi, i, h: (bi, jnp.maximum(i * (ts // SUBLANES) - 1, 0), off + h))
    cw = lambda off: pl.BlockSpec((CONV_WIDTH, HEAD_DIM), lambda bi, i, h: (0, off + h))
    vec = pl.BlockSpec((1, LANES), lambda bi, i, h: (0, 0))
    out_tile = pl.BlockSpec((1, ts, HEAD_DIM), lambda bi, i, h: (bi, i, h))
    big = lambda dt: jax.ShapeDtypeStruct((b, seq, DELTA_WIDTH), dt)
    return pl.pallas_call(
        functools.partial(_delta_prep_kernel, ts=ts),
        grid=(b, seq // ts, hd),
        in_specs=[pl.BlockSpec((1, ts, LANES), lambda bi, i, h: (bi, i, 0)),
                  vec, vec,
                  pl.BlockSpec((1, 3 * LANES, 2 * LANES), lambda bi, i, h: (h, 0, 0)),
                  pl.BlockSpec((1, PAIR, 3 * LANES), lambda bi, i, h: (h, 0, 0)),
                  cw(0), cw(hd), cw(2 * hd),
                  tile(q0), tile(q0 + hd), tile(q0 + 2 * hd),
                  halo(q0), halo(q0 + hd), halo(q0 + 2 * hd)],
        out_specs=[out_tile, out_tile, out_tile, out_tile, out_tile,
                   pl.BlockSpec((1, 1, ts // CHUNK, HEAD_DIM), lambda bi, i, h: (bi, h, i, 0))],
        out_shape=[big(F32), big(BF16), big(BF16), big(BF16), big(BF16),
                   jax.ShapeDtypeStruct((b, hd, seq // CHUNK, HEAD_DIM), F32)],
        scratch_shapes=[pltpu.VMEM((SUBLANES + ts, HEAD_DIM), F32),
                        pltpu.VMEM((ts, 3 * LANES), BF16)],
        compiler_params=_cparams(("parallel", "parallel", "arbitrary")),
        name="delta_prep",
    )(ab, alog_vec, dtb_vec, sel, selr, conv_w, conv_w, conv_w, proj, proj, proj, proj, proj, proj)


def _selectors():
    hd = N_DELTA_HEADS
    src = jnp.arange(3 * LANES) % LANES
    dst = jnp.arange(2 * LANES) // LANES
    head = jnp.arange(hd)[:, None, None]
    sel = (src[None, :, None] == head + hd * dst[None, None, :]).astype(BF16)
    selr = jnp.broadcast_to((src[None, None, :] == head).astype(BF16), (hd, PAIR, 3 * LANES))
    return sel, selr


def _delta_scan_kernel(u_ref, w_ref, qd_ref, kt_ref, in_ref, dec_ref, z_ref, gain_ref, o_ref, s_ref,
                       *, nb, npair):
    @pl.when(pl.program_id(0) == 0)
    def _():
        s_ref[...] = jnp.zeros_like(s_ref)

    gain = gain_ref[...]

    chains = [(bi, h) for bi in range(nb) for h in range(N_DELTA_HEADS)]
    cols = lambda h: slice(h * HEAD_DIM, (h + 1) * HEAD_DIM)

    def pair(pi, carry):
        for e in range(2):
            sl = pl.ds(pl.multiple_of(pi * PAIR + e * CHUNK, CHUNK), CHUNK)
            st = [s_ref[bi * N_DELTA_HEADS + h] for bi, h in chains]
            sb = [s.astype(BF16) for s in st]
            ws = [jnp.dot(w_ref[bi, sl, cols(h)], sb[n], preferred_element_type=F32)
                  for n, (bi, h) in enumerate(chains)]
            qs = [jnp.dot(qd_ref[bi, sl, cols(h)], sb[n], preferred_element_type=F32)
                  for n, (bi, h) in enumerate(chains)]
            vb = [(u_ref[bi, sl, cols(h)] - ws[n]).astype(BF16) for n, (bi, h) in enumerate(chains)]
            zero = jnp.zeros((CHUNK, HEAD_DIM), BF16)
            vpair = [jnp.concatenate([x, zero] if e == 0 else [zero, x], axis=0) for x in vb]
            o = [qs[n] + jnp.dot(in_ref[bi, sl, cols(h)], vpair[n], preferred_element_type=F32)
                 for n, (bi, h) in enumerate(chains)]
            for n, (bi, h) in enumerate(chains):
                dec = dec_ref[bi, h, pl.ds(2 * pi + e, 1), :]
                s_ref[bi * N_DELTA_HEADS + h] = st[n] * dec + lax.dot_general(
                    kt_ref[bi, sl, cols(h)], vb[n], TN_DIMS, preferred_element_type=F32)
            for n, (bi, h) in enumerate(chains):
                y = o[n] * lax.rsqrt(jnp.mean(o[n] * o[n], axis=-1, keepdims=True) + NORM_EPS) * gain
                z = z_ref[bi, sl, cols(h)]
                o_ref[bi, sl, cols(h)] = (y * (z * jax.nn.sigmoid(z))).astype(o_ref.dtype)
        return carry

    lax.fori_loop(0, npair, pair, 0)


def _delta_scan(u, w, qd, kt, intra, dec, proj, gain):
    b, seq, _ = u.shape
    cg = SUBLANES
    rows = cg * CHUNK
    zblk = (3 * ATTN_WIDTH + 3 * DELTA_WIDTH) // DELTA_WIDTH
    tile = pl.BlockSpec((b, rows, DELTA_WIDTH), lambda i: (0, i, 0))
    return pl.pallas_call(
        functools.partial(_delta_scan_kernel, nb=b, npair=rows // PAIR),
        grid=(seq // rows,),
        in_specs=[tile, tile, tile, tile, tile,
                  pl.BlockSpec((b, N_DELTA_HEADS, cg, HEAD_DIM), lambda i: (0, 0, i, 0)),
                  pl.BlockSpec((b, rows, DELTA_WIDTH), lambda i: (0, i, zblk)),
                  pl.BlockSpec((1, HEAD_DIM), lambda i: (0, 0))],
        out_specs=tile,
        out_shape=jax.ShapeDtypeStruct((b, seq, DELTA_WIDTH), BF16),
        scratch_shapes=[pltpu.VMEM((b * N_DELTA_HEADS, HEAD_DIM, HEAD_DIM), F32)],
        compiler_params=_cparams(("arbitrary",)),
        name="delta_scan",
    )(u, w, qd, kt, intra, dec, proj, gain)


def _outproj_kernel(oa_ref, od_ref, wa_ref, wd_ref, x_ref, mod_ref, o_ref):
    y = (jnp.dot(oa_ref[...], wa_ref[...].astype(BF16), preferred_element_type=F32)
         + jnp.dot(od_ref[...], wd_ref[...].astype(BF16), preferred_element_type=F32))
    o_ref[...] = x_ref[...] + mod_ref[0, 5:6, :] * y


def _outproj(oa, od, w_out, x2d, mod, layer, *, seq):
    t, d = x2d.shape
    tm = _pick(seq, (1024, 512, 256, 128))
    tn = _pick(d, (1024, 512, 256, 128))
    assert ATTN_WIDTH == DELTA_WIDTH
    return pl.pallas_call(
        _outproj_kernel,
        grid=(t // tm, d // tn),
        in_specs=[pl.BlockSpec((tm, ATTN_WIDTH), lambda i, j: (i, 0)),
                  pl.BlockSpec((tm, DELTA_WIDTH), lambda i, j: (i, 0)),
                  pl.BlockSpec((None, ATTN_WIDTH, tn), lambda i, j: (layer, 0, j)),
                  pl.BlockSpec((None, DELTA_WIDTH, tn), lambda i, j: (layer, 1, j)),
                  pl.BlockSpec((tm, tn), lambda i, j: (i, j)),
                  pl.BlockSpec((1, N_MOD, tn), lambda i, j: ((i * tm) // seq, 0, j))],
        out_specs=pl.BlockSpec((tm, tn), lambda i, j: (i, j)),
        out_shape=jax.ShapeDtypeStruct((t, d), F32),
        compiler_params=_cparams(("parallel", "parallel")),
        name="outproj",
    )(oa, od, w_out, w_out, x2d, mod)


def _layer(layer, x, mod, positions, ffn1_norm, ffn1_w_gate, ffn1_w_up, ffn1_w_down, mix_norm, w_in, conv_w,
           q_norm, k_norm, a_log, dt_bias, delta_out_norm, w_out, ffn2_norm, ffn2_w_gate, ffn2_w_up,
           ffn2_w_down):
    b, seq, d = x.shape
    t = b * seq
    x2d = x.reshape(t, d)
    row = lambda v: v.reshape(1, -1).astype(F32)

    x2d = _ffn(x2d, mod, row(ffn1_norm), ffn1_w_gate, ffn1_w_up, ffn1_w_down, layer, sub=0, seq=seq)

    n_ab = w_in.shape[2] - MAIN_PROJ
    w_ab = jnp.pad(w_in[layer, :, MAIN_PROJ:], ((0, 0), (0, LANES - n_ab))).astype(BF16)
    proj, ab = _inproj(x2d, mod, row(mix_norm), w_in, w_ab, layer, seq=seq)
    proj = proj.reshape(b, seq, MAIN_PROJ)
    ab = ab.reshape(b, seq, LANES)

    lane = jnp.arange(HEAD_DIM)
    inv_freq = ROPE_THETA ** (-(lane % ROPE_HALF).astype(F32) / ROPE_HALF)
    invf = jnp.where(lane < 2 * ROPE_HALF, inv_freq, 0.0).reshape(1, HEAD_DIM).astype(F32)
    sgn = jnp.where(lane < ROPE_HALF, -1.0, jnp.where(lane < 2 * ROPE_HALF, 1.0, 0.0)).reshape(1, HEAD_DIM)
    pos_b = jnp.broadcast_to(positions.astype(F32)[..., None], (b, seq, HEAD_DIM))
    oa = _attention(proj, pos_b, invf, sgn.astype(F32), row(q_norm), row(k_norm))

    lane_vec = lambda v: jnp.pad(v.astype(F32), (0, LANES - N_DELTA_HEADS)).reshape(1, LANES)
    sel, selr = _selectors()
    u, w, qd, kt, intra, dec = _delta_prep(proj, ab, lane_vec(a_log), lane_vec(dt_bias), sel, selr,
                                           conv_w.astype(F32))
    od = _delta_scan(u, w, qd, kt, intra, dec, proj, row(delta_out_norm))

    x2d = _outproj(oa.reshape(t, ATTN_WIDTH), od.reshape(t, DELTA_WIDTH), w_out, x2d, mod, layer, seq=seq)

    x2d = _ffn(x2d, mod, row(ffn2_norm), ffn2_w_gate, ffn2_w_up, ffn2_w_down, layer, sub=2, seq=seq)
    return x2d.reshape(b, seq, d)


def kernel(x, c, positions, w_ada, b_ada, ffn1_norm, ffn1_w_gate, ffn1_w_up, ffn1_w_down, mix_norm, w_in,
           conv_w, q_norm, k_norm, a_log, dt_bias, delta_out_norm, w_out, ffn2_norm, ffn2_w_gate, ffn2_w_up,
           ffn2_w_down):
    b, _, d = x.shape
    depth = w_ada.shape[0]
    c8 = jnp.pad(c.astype(F32), ((0, SUBLANES - b), (0, 0)))
    for l in range(depth):
        mod = _adaln(c8, w_ada[l], b_ada[l].reshape(1, -1))[:b].reshape(b, N_MOD, d)
        x = _layer(l, x, mod, positions, ffn1_norm[l], ffn1_w_gate, ffn1_w_up, ffn1_w_down,
                   mix_norm[l], w_in, conv_w[l], q_norm[l], k_norm[l], a_log[l], dt_bias[l],
                   delta_out_norm[l], w_out, ffn2_norm[l], ffn2_w_gate, ffn2_w_up, ffn2_w_down)
    return x
```

```python
import functools

import jax
import jax.numpy as jnp
from jax import lax
from jax.experimental import pallas as pl
from jax.experimental.pallas import tpu as pltpu

F32 = jnp.float32
BF16 = jnp.bfloat16

HEAD_DIM = 128
N_ATTN_HEADS = 8
N_DELTA_HEADS = 8
ATTN_WIDTH = N_ATTN_HEADS * HEAD_DIM
DELTA_WIDTH = N_DELTA_HEADS * HEAD_DIM
MIX_WIDTH = ATTN_WIDTH + DELTA_WIDTH
DILATIONS = (1, 4, 16)
Q_BLOCK = 128
ATTN_UNROLL = 4
ROPE_THETA = 500000.0
ROPE_HALF = HEAD_DIM // 8
CONV_WIDTH = 4
CHUNK = 64
PAIR = 2 * CHUNK
NORM_EPS = 1e-6
N_MOD = 9
MAIN_PROJ = 3 * ATTN_WIDTH + 4 * DELTA_WIDTH
LANES = 128
SUBLANES = 8
VMEM_LIMIT = 60 * 1024 * 1024
NEG_INF = float("-inf")
NT_DIMS = (((1,), (1,)), ((), ()))
TN_DIMS = (((0,), (0,)), ((), ()))

assert PAIR == LANES


def _cparams(semantics):
    return pltpu.CompilerParams(dimension_semantics=semantics, vmem_limit_bytes=VMEM_LIMIT)


def _pick(n, candidates):
    for c in candidates:
        if n % c == 0:
            return c
    return n


def _split_bf16(x, terms):
    parts = []
    for _ in range(terms):
        p = x.astype(BF16)
        parts.append(p)
        x = x - p.astype(F32)
    return parts


def _adaln_kernel(c_ref, w_ref, b_ref, o_ref):
    c = c_ref[...]
    ca = (c * jax.nn.sigmoid(c)).astype(BF16)
    o_ref[...] = jnp.dot(ca, w_ref[...].astype(BF16), preferred_element_type=F32) + b_ref[...]


def _adaln(c8, w, b):
    d, n = w.shape
    tn = _pick(n, (1024, 512, 256, 128))
    return pl.pallas_call(
        _adaln_kernel,
        grid=(n // tn,),
        in_specs=[pl.BlockSpec((SUBLANES, d), lambda j: (0, 0)),
                  pl.BlockSpec((d, tn), lambda j: (0, j)),
                  pl.BlockSpec((1, tn), lambda j: (0, j))],
        out_specs=pl.BlockSpec((SUBLANES, tn), lambda j: (0, j)),
        out_shape=jax.ShapeDtypeStruct((SUBLANES, n), F32),
        compiler_params=_cparams(("parallel",)),
        name="adaln",
    )(c8, w, b)


def _modulated(x, gain, mod_ref, sub):
    ms = jnp.mean(x * x, axis=-1, keepdims=True)
    y = x * lax.rsqrt(ms + NORM_EPS) * gain
    return y * (1.0 + mod_ref[0, 3 * sub + 1:3 * sub + 2, :]) + mod_ref[0, 3 * sub:3 * sub + 1, :]


def _ffn_kernel(x_ref, mod_ref, gain_ref, wg_ref, wu_ref, wd_ref, o_ref, h_ref, *, sub, nf, nsplit):
    f = pl.program_id(1)

    @pl.when(f == 0)
    def _():
        h_ref[...] = _modulated(x_ref[...], gain_ref[...], mod_ref, sub).astype(BF16)
        o_ref[...] = jnp.zeros_like(o_ref)

    h = h_ref[...]
    g = jnp.dot(h, wg_ref[...].astype(BF16), preferred_element_type=F32)
    u = jnp.dot(h, wu_ref[...].astype(BF16), preferred_element_type=F32)
    a = (g * jax.nn.sigmoid(g) * u).astype(BF16)
    wd = wd_ref[...].astype(BF16)
    dn = o_ref.shape[1] // nsplit
    for n in range(nsplit):
        cs = slice(n * dn, (n + 1) * dn)
        o_ref[:, cs] += jnp.dot(a, wd[:, cs], preferred_element_type=F32)

    @pl.when(f == nf - 1)
    def _():
        gate = 0.5 * mod_ref[0, 3 * sub + 2:3 * sub + 3, :]
        o_ref[...] = x_ref[...] + gate * o_ref[...]


def _ffn(x2d, mod, gain, wg, wu, wd, layer, *, sub, seq):
    t, d = x2d.shape
    dff = wg.shape[2]
    tm = _pick(seq, (1024, 512, 256, 128))
    tf = _pick(dff, (256, 128))
    nf = dff // tf
    return pl.pallas_call(
        functools.partial(_ffn_kernel, sub=sub, nf=nf, nsplit=d // _pick(d, (512, 256, 128))),
        grid=(t // tm, nf),
        in_specs=[pl.BlockSpec((tm, d), lambda i, f: (i, 0)),
                  pl.BlockSpec((1, N_MOD, d), lambda i, f: ((i * tm) // seq, 0, 0)),
                  pl.BlockSpec((1, d), lambda i, f: (0, 0)),
                  pl.BlockSpec((None, d, tf), lambda i, f: (layer, 0, f)),
                  pl.BlockSpec((None, d, tf), lambda i, f: (layer, 0, f)),
                  pl.BlockSpec((None, tf, d), lambda i, f: (layer, f, 0))],
        out_specs=pl.BlockSpec((tm, d), lambda i, f: (i, 0)),
        out_shape=jax.ShapeDtypeStruct((t, d), F32),
        scratch_shapes=[pltpu.VMEM((tm, d), BF16)],
        compiler_params=_cparams(("parallel", "arbitrary")),
        name=f"ffn{sub}",
    )(x2d, mod, gain, wg, wu, wd)


def _inproj_kernel(x_ref, mod_ref, gain_ref, w_ref, wab_ref, o_ref, ab_ref, h_ref):
    n = pl.program_id(1)

    @pl.when(n == 0)
    def _():
        h = _modulated(x_ref[...], gain_ref[...], mod_ref, 1).astype(BF16)
        h_ref[...] = h
        ab_ref[...] = jnp.dot(h, wab_ref[...], preferred_element_type=F32)

    o_ref[...] = lax.dot_general(h_ref[...], w_ref[...].astype(BF16), NT_DIMS, preferred_element_type=F32)


def _inproj(x2d, mod, gain, w_in_t, w_ab, layer, *, seq):
    t, d = x2d.shape
    n = MAIN_PROJ
    tm = _pick(seq, (1024, 512, 256, 128))
    tn = _pick(n, (1024, 512, 256, 128))
    return pl.pallas_call(
        _inproj_kernel,
        grid=(t // tm, n // tn),
        in_specs=[pl.BlockSpec((tm, d), lambda i, j: (i, 0)),
                  pl.BlockSpec((1, N_MOD, d), lambda i, j: ((i * tm) // seq, 0, 0)),
                  pl.BlockSpec((1, d), lambda i, j: (0, 0)),
                  pl.BlockSpec((None, tn, d), lambda i, j: (layer, j, 0)),
                  pl.BlockSpec((d, LANES), lambda i, j: (0, 0))],
        out_specs=[pl.BlockSpec((tm, tn), lambda i, j: (i, j)),
                   pl.BlockSpec((tm, LANES), lambda i, j: (i, 0))],
        out_shape=[jax.ShapeDtypeStruct((t, n), F32), jax.ShapeDtypeStruct((t, LANES), F32)],
        scratch_shapes=[pltpu.VMEM((tm, d), BF16)],
        compiler_params=_cparams(("parallel", "arbitrary")),
        name="inproj",
    )(x2d, mod, gain, w_in_t, w_ab)


def _attn_kernel(pos_ref, invf_ref, sgn_ref, qg_ref, kg_ref, q_ref, k_ref, v_ref, o_ref,
                 cos_ref, sin_ref, qn_ref, kp_ref, vp_ref, ob_ref, lb_ref, bias_ref, *, seq, pad):
    h = pl.program_id(1)
    nb = len(DILATIONS)

    @pl.when(h == 0)
    def _():
        ang = pos_ref[0] * invf_ref[...]
        cos_ref[...] = jnp.cos(ang)
        sin_ref[...] = jnp.sin(ang) * sgn_ref[...]

    lane = lax.broadcasted_iota(jnp.int32, (seq, HEAD_DIM), 1)

    def norm_rope(x, gain):
        y = x * lax.rsqrt(jnp.mean(x * x, axis=-1, keepdims=True) + NORM_EPS) * gain
        partner = jnp.where(lane < ROPE_HALF, pltpu.roll(y, HEAD_DIM - ROPE_HALF, 1),
                            pltpu.roll(y, ROPE_HALF, 1))
        return y * cos_ref[...] + partner * sin_ref[...]

    qn_ref[...] = norm_rope(q_ref[0], qg_ref[...]) * (HEAD_DIM ** -0.5)
    kp_ref[0:pad, :] = jnp.zeros((pad, HEAD_DIM), F32)
    vp_ref[0:pad, :] = jnp.zeros((pad, HEAD_DIM), F32)
    kp_ref[pad:pad + seq, :] = norm_rope(k_ref[0], kg_ref[...])
    vp_ref[pad:pad + seq, :] = v_ref[0]

    qi = lax.broadcasted_iota(jnp.int32, (Q_BLOCK, 2 * Q_BLOCK), 0)
    kj = lax.broadcasted_iota(jnp.int32, (Q_BLOCK, 2 * Q_BLOCK), 1)
    dist = qi + Q_BLOCK - kj
    band = (dist >= 0) & (dist <= Q_BLOCK)
    bias_ref[0] = jnp.where(band, 0.0, NEG_INF)
    bias_ref[1] = jnp.where(band & (kj >= Q_BLOCK), 0.0, NEG_INF)

    nblocks = seq // Q_BLOCK
    for p, d in enumerate(DILATIONS):
        nblk = nblocks // d

        def rows(start, size, d=d):
            return pl.ds(start, size) if d == 1 else pl.ds(start, size, stride=d)

        def group(it, carry, d=d, nblk=nblk, p=p, rows=rows):
            qstart, first = [], []
            for j in range(ATTN_UNROLL):
                t = it * ATTN_UNROLL + j
                r = lax.div(t, nblk)
                n = lax.rem(t, nblk)
                qstart.append(r + d * Q_BLOCK * n)
                first.append(jnp.where(n == 0, 1, 0))
            q = [qn_ref[rows(qs, Q_BLOCK), :].astype(BF16) for qs in qstart]
            k = [kp_ref[rows(pad + qs - d * Q_BLOCK, 2 * Q_BLOCK), :].astype(BF16) for qs in qstart]
            v = [vp_ref[rows(pad + qs - d * Q_BLOCK, 2 * Q_BLOCK), :].astype(BF16) for qs in qstart]
            s = [lax.dot_general(q[j], k[j], NT_DIMS, preferred_element_type=F32) + bias_ref[first[j]]
                 for j in range(ATTN_UNROLL)]
            m = [jnp.max(x, axis=-1, keepdims=True) for x in s]
            e = [jnp.exp(s[j] - m[j]) for j in range(ATTN_UNROLL)]
            l = [jnp.sum(x, axis=-1, keepdims=True) for x in e]
            acc = [jnp.dot(e[j].astype(BF16), v[j], preferred_element_type=F32) for j in range(ATTN_UNROLL)]
            for j in range(ATTN_UNROLL):
                ob_ref[rows(p * seq + qstart[j], Q_BLOCK), :] = acc[j] / l[j]
                lb_ref[rows(p * seq + qstart[j], Q_BLOCK), :] = jnp.broadcast_to(
                    m[j] + jnp.log(l[j]), (Q_BLOCK, HEAD_DIM))
            return carry

        lax.fori_loop(0, nblocks // ATTN_UNROLL, group, 0)

    lses = [lb_ref[p * seq:(p + 1) * seq, :] for p in range(nb)]
    mx = functools.reduce(jnp.maximum, lses)
    ws = [jnp.exp(l - mx) for l in lses]
    num = functools.reduce(lambda a, b: a + b, [w * ob_ref[p * seq:(p + 1) * seq, :] for p, w in enumerate(ws)])
    den = functools.reduce(lambda a, b: a + b, ws)
    o_ref[0] = (num / den).astype(o_ref.dtype)


def _attention(proj, pos_b, invf, sgn, qg, kg):
    b, seq, _ = proj.shape
    pad = DILATIONS[-1] * Q_BLOCK
    ha = N_ATTN_HEADS
    blk = lambda off: pl.BlockSpec((1, seq, HEAD_DIM), lambda i, h: (i, 0, off + h))
    vec = pl.BlockSpec((1, HEAD_DIM), lambda i, h: (0, 0))
    return pl.pallas_call(
        functools.partial(_attn_kernel, seq=seq, pad=pad),
        grid=(b, ha),
        in_specs=[pl.BlockSpec((1, seq, HEAD_DIM), lambda i, h: (i, 0, 0)),
                  vec, vec, vec, vec, blk(0), blk(ha), blk(2 * ha)],
        out_specs=pl.BlockSpec((1, seq, HEAD_DIM), lambda i, h: (i, 0, h)),
        out_shape=jax.ShapeDtypeStruct((b, seq, ATTN_WIDTH), BF16),
        scratch_shapes=[pltpu.VMEM((seq, HEAD_DIM), F32),
                        pltpu.VMEM((seq, HEAD_DIM), F32),
                        pltpu.VMEM((seq, HEAD_DIM), F32),
                        pltpu.VMEM((pad + seq, HEAD_DIM), F32),
                        pltpu.VMEM((pad + seq, HEAD_DIM), F32),
                        pltpu.VMEM((len(DILATIONS) * seq, HEAD_DIM), F32),
                        pltpu.VMEM((len(DILATIONS) * seq, HEAD_DIM), F32),
                        pltpu.VMEM((2, Q_BLOCK, 2 * Q_BLOCK), F32)],
        compiler_params=_cparams(("parallel", "arbitrary")),
        name="dilated_attn",
    )(pos_b, invf, sgn, qg, kg, proj, proj, proj)


def _softplus(x):
    return jnp.maximum(x, 0.0) + jnp.log1p(jnp.exp(-jnp.abs(x)))


def _delta_prep_kernel(ab_ref, alog_ref, dtb_ref, sel_ref, selr_ref, cwq_ref, cwk_ref, cwv_ref,
                       q_ref, k_ref, v_ref, hq_ref, hk_ref, hv_ref,
                       u_ref, w_ref, qd_ref, kt_ref, in_ref, dec_ref,
                       xp_ref, comp_ref, *, ts):
    i = pl.program_id(1)
    h = pl.program_id(2)
    npair = ts // PAIR
    ci = lax.broadcasted_iota(jnp.int32, (PAIR, PAIR), 0)
    cj = lax.broadcasted_iota(jnp.int32, (PAIR, PAIR), 1)
    same = (ci // CHUNK) == (cj // CHUNK)
    causal = same & (ci >= cj)
    strict = same & (ci > cj)
    eye = (ci == cj).astype(F32)

    @pl.when(h == 0)
    def _():
        lane = lax.broadcasted_iota(jnp.int32, (PAIR, LANES), 1)
        tril = causal.astype(F32)
        for p in range(npair):
            rs = slice(p * PAIR, (p + 1) * PAIR)
            ab = ab_ref[0, rs, :]
            g = -jnp.exp(alog_ref[...]) * _softplus(ab + dtb_ref[...])
            gc = jnp.dot(tril, g, preferred_element_type=F32, precision=lax.Precision.HIGHEST)
            comp = jnp.where(lane < N_DELTA_HEADS, gc, jax.nn.sigmoid(ab))
            for t, part in enumerate(_split_bf16(comp, 3)):
                comp_ref[rs, t * LANES:(t + 1) * LANES] = part

    def conv_silu(x_ref, halo_ref, cw_ref):
        halo = halo_ref[0]
        xp_ref[0:SUBLANES, :] = jnp.where(i > 0, halo, jnp.zeros_like(halo))
        xp_ref[SUBLANES:SUBLANES + ts, :] = x_ref[0]
        y = jnp.zeros((ts, HEAD_DIM), F32)
        for j in range(CONV_WIDTH):
            off = SUBLANES - (CONV_WIDTH - 1) + j
            y = y + cw_ref[j:j + 1, :] * xp_ref[off:off + ts, :]
        return y * jax.nn.sigmoid(y)

    ones2 = jnp.ones((2 * HEAD_DIM, HEAD_DIM), BF16)

    def l2n(x):
        ss = jnp.dot(jnp.concatenate(_split_bf16(x * x, 2), axis=1), ones2, preferred_element_type=F32)
        return x * lax.rsqrt(ss + NORM_EPS)

    q = l2n(conv_silu(q_ref, hq_ref, cwq_ref)) * (HEAD_DIM ** -0.5)
    k = l2n(conv_silu(k_ref, hk_ref, cwk_ref))
    v = conv_silu(v_ref, hv_ref, cwv_ref)
    rep = jnp.dot(comp_ref[...], sel_ref[0], preferred_element_type=F32)

    merge_masks = []
    s = 1
    while s < CHUNK:
        merge_masks.append(((ci // s) % 2 == 1) & ((cj // s) == (ci // s) - 1))
        s *= 2
    row = lax.broadcasted_iota(jnp.int32, (PAIR, LANES), 0)

    pairs = range(npair)
    rsl = [slice(p * PAIR, (p + 1) * PAIR) for p in pairs]
    gc = [rep[rs, :LANES] for rs in rsl]
    beta = [rep[rs, LANES:] for rs in rsl]
    grow = [lax.dot_general(selr_ref[0], comp_ref[rs, :], NT_DIMS, preferred_element_type=F32)
            for rs in rsl]
    gamma = [jnp.exp(jnp.where(causal, gc[p] - grow[p], NEG_INF)) for p in pairs]
    kb = [k[rsl[p]] * beta[p] for p in pairs]
    kbf = [k[rs].astype(BF16) for rs in rsl]
    kk = [lax.dot_general(kb[p].astype(BF16), kbf[p], NT_DIMS, preferred_element_type=F32) for p in pairs]
    a = [jnp.where(strict, kk[p] * gamma[p], 0.0) for p in pairs]
    t = [eye - jnp.where(merge_masks[0], a[p], 0.0) for p in pairs]
    for off in merge_masks[1:]:
        tb = [t[p].astype(BF16) for p in pairs]
        x = [jnp.dot(jnp.where(off, a[p], 0.0).astype(BF16), tb[p], preferred_element_type=F32).astype(BF16)
             for p in pairs]
        t = [t[p] - jnp.dot(tb[p], x[p], preferred_element_type=F32) for p in pairs]
    eg = [jnp.exp(gc[p]) for p in pairs]
    uw = [jnp.dot(t[p].astype(BF16),
                  jnp.concatenate([v[rsl[p]] * beta[p], kb[p] * eg[p]], axis=1).astype(BF16),
                  preferred_element_type=F32) for p in pairs]
    for p in pairs:
        rs = rsl[p]
        intra = lax.dot_general(q[rs].astype(BF16), kbf[p], NT_DIMS, preferred_element_type=F32) * gamma[p]
        g0 = gc[p][CHUNK - 1:CHUNK, :]
        g1 = gc[p][PAIR - 1:PAIR, :]
        glast = jnp.where(row < CHUNK, g0, g1)
        u_ref[0, rs, :] = uw[p][:, :HEAD_DIM]
        w_ref[0, rs, :] = uw[p][:, HEAD_DIM:].astype(BF16)
        qd_ref[0, rs, :] = (q[rs] * eg[p]).astype(BF16)
        kt_ref[0, rs, :] = (k[rs] * jnp.exp(glast - gc[p])).astype(BF16)
        in_ref[0, rs, :] = intra.astype(BF16)
        dec_ref[0, 0, 2 * p:2 * p + 1, :] = jnp.exp(g0)
        dec_ref[0, 0, 2 * p + 1:2 * p + 2, :] = jnp.exp(g1)


def _delta_prep(proj, ab, alog_vec, dtb_vec, sel, selr, conv_w):
    b, seq, _ = proj.shape
    hd = N_DELTA_HEADS
    ts = _pick(seq, (1024, 512, 256, 128))
    q0 = 3 * N_ATTN_HEADS
    tile = lambda off: pl.BlockSpec((1, ts, HEAD_DIM), lambda bi, i, h: (bi, i, off + h))
    halo = lambda off: pl.BlockSpec(
        (1, SUBLANES, HEAD_DIM),
        lambda bi, i, h: (bi, jnp.maximum(i * (ts // SUBLANES) - 1, 0), off + h))
    cw = lambda off: pl.BlockSpec((CONV_WIDTH, HEAD_DIM), lambda bi, i, h: (0, off + h))
    vec = pl.BlockSpec((1, LANES), lambda bi, i, h: (0, 0))
    out_tile = pl.BlockSpec((1, ts, HEAD_DIM), lambda bi, i, h: (bi, i, h))
    big = lambda dt: jax.ShapeDtypeStruct((b, seq, DELTA_WIDTH), dt)
    return pl.pallas_call(
        functools.partial(_delta_prep_kernel, ts=ts),
        grid=(b, seq // ts, hd),
        in_specs=[pl.BlockSpec((1, ts, LANES), lambda bi, i, h: (bi, i, 0)),
                  vec, vec,
                  pl.BlockSpec((1, 3 * LANES, 2 * LANES), lambda bi, i, h: (h, 0, 0)),
                  pl.BlockSpec((1, PAIR, 3 * LANES), lambda bi, i, h: (h, 0, 0)),
                  cw(0), cw(hd), cw(2 * hd),
                  tile(q0), tile(q0 + hd), tile(q0 + 2 * hd),
                  halo(q0), halo(q0 + hd), halo(q0 + 2 * hd)],
        out_specs=[out_tile, out_tile, out_tile, out_tile, out_tile,
                   pl.BlockSpec((1, 1, ts // CHUNK, HEAD_DIM), lambda bi, i, h: (bi, h, i, 0))],
        out_shape=[big(F32), big(BF16), big(BF16), big(BF16), big(BF16),
                   jax.ShapeDtypeStruct((b, hd, seq // CHUNK, HEAD_DIM), F32)],
        scratch_shapes=[pltpu.VMEM((SUBLANES + ts, HEAD_DIM), F32),
                        pltpu.VMEM((ts, 3 * LANES), BF16)],
        compiler_params=_cparams(("parallel", "parallel", "arbitrary")),
        name="delta_prep",
    )(ab, alog_vec, dtb_vec, sel, selr, conv_w, conv_w, conv_w, proj, proj, proj, proj, proj, proj)


def _selectors():
    hd = N_DELTA_HEADS
    src = jnp.arange(3 * LANES) % LANES
    dst = jnp.arange(2 * LANES) // LANES
    head = jnp.arange(hd)[:, None, None]
    sel = (src[None, :, None] == head + hd * dst[None, None, :]).astype(BF16)
    selr = jnp.broadcast_to((src[None, None, :] == head).astype(BF16), (hd, PAIR, 3 * LANES))
    return sel, selr


def _delta_scan_kernel(u_ref, w_ref, qd_ref, kt_ref, in_ref, dec_ref, z_ref, gain_ref, o_ref, s_ref,
                       *, nb, npair):
    @pl.when(pl.program_id(0) == 0)
    def _():
        s_ref[...] = jnp.zeros_like(s_ref)

    gain = gain_ref[...]

    chains = [(bi, h) for bi in range(nb) for h in range(N_DELTA_HEADS)]
    cols = lambda h: slice(h * HEAD_DIM, (h + 1) * HEAD_DIM)

    def pair(pi, carry):
        for e in range(2):
            sl = pl.ds(pl.multiple_of(pi * PAIR + e * CHUNK, CHUNK), CHUNK)
            st = [s_ref[bi * N_DELTA_HEADS + h] for bi, h in chains]
            sb = [s.astype(BF16) for s in st]
            ws = [jnp.dot(w_ref[bi, sl, cols(h)], sb[n], preferred_element_type=F32)
                  for n, (bi, h) in enumerate(chains)]
            qs = [jnp.dot(qd_ref[bi, sl, cols(h)], sb[n], preferred_element_type=F32)
                  for n, (bi, h) in enumerate(chains)]
            vb = [(u_ref[bi, sl, cols(h)] - ws[n]).astype(BF16) for n, (bi, h) in enumerate(chains)]
            zero = jnp.zeros((CHUNK, HEAD_DIM), BF16)
            vpair = [jnp.concatenate([x, zero] if e == 0 else [zero, x], axis=0) for x in vb]
            o = [qs[n] + jnp.dot(in_ref[bi, sl, cols(h)], vpair[n], preferred_element_type=F32)
                 for n, (bi, h) in enumerate(chains)]
            for n, (bi, h) in enumerate(chains):
                dec = dec_ref[bi, h, pl.ds(2 * pi + e, 1), :]
                s_ref[bi * N_DELTA_HEADS + h] = st[n] * dec + lax.dot_general(
                    kt_ref[bi, sl, cols(h)], vb[n], TN_DIMS, preferred_element_type=F32)
            for n, (bi, h) in enumerate(chains):
                y = o[n] * lax.rsqrt(jnp.mean(o[n] * o[n], axis=-1, keepdims=True) + NORM_EPS) * gain
                z = z_ref[bi, sl, cols(h)]
                o_ref[bi, sl, cols(h)] = (y * (z * jax.nn.sigmoid(z))).astype(o_ref.dtype)
        return carry

    lax.fori_loop(0, npair, pair, 0)


def _delta_scan(u, w, qd, kt, intra, dec, proj, gain):
    b, seq, _ = u.shape
    cg = SUBLANES
    rows = cg * CHUNK
    zblk = (3 * ATTN_WIDTH + 3 * DELTA_WIDTH) // DELTA_WIDTH
    tile = pl.BlockSpec((b, rows, DELTA_WIDTH), lambda i: (0, i, 0))
    return pl.pallas_call(
        functools.partial(_delta_scan_kernel, nb=b, npair=rows // PAIR),
        grid=(seq // rows,),
        in_specs=[tile, tile, tile, tile, tile,
                  pl.BlockSpec((b, N_DELTA_HEADS, cg, HEAD_DIM), lambda i: (0, 0, i, 0)),
                  pl.BlockSpec((b, rows, DELTA_WIDTH), lambda i: (0, i, zblk)),
                  pl.BlockSpec((1, HEAD_DIM), lambda i: (0, 0))],
        out_specs=tile,
        out_shape=jax.ShapeDtypeStruct((b, seq, DELTA_WIDTH), BF16),
        scratch_shapes=[pltpu.VMEM((b * N_DELTA_HEADS, HEAD_DIM, HEAD_DIM), F32)],
        compiler_params=_cparams(("arbitrary",)),
        name="delta_scan",
    )(u, w, qd, kt, intra, dec, proj, gain)


def _outproj_kernel(oa_ref, od_ref, wa_ref, wd_ref, x_ref, mod_ref, o_ref):
    y = (jnp.dot(oa_ref[...], wa_ref[...].astype(BF16), preferred_element_type=F32)
         + jnp.dot(od_ref[...], wd_ref[...].astype(BF16), preferred_element_type=F32))
    o_ref[...] = x_ref[...] + mod_ref[0, 5:6, :] * y


def _outproj(oa, od, w_out, x2d, mod, layer, *, seq):
    t, d = x2d.shape
    tm = _pick(seq, (1024, 512, 256, 128))
    tn = _pick(d, (1024, 512, 256, 128))
    assert ATTN_WIDTH == DELTA_WIDTH
    return pl.pallas_call(
        _outproj_kernel,
        grid=(t // tm, d // tn),
        in_specs=[pl.BlockSpec((tm, ATTN_WIDTH), lambda i, j: (i, 0)),
                  pl.BlockSpec((tm, DELTA_WIDTH), lambda i, j: (i, 0)),
                  pl.BlockSpec((None, ATTN_WIDTH, tn), lambda i, j: (layer, 0, j)),
                  pl.BlockSpec((None, DELTA_WIDTH, tn), lambda i, j: (layer, 1, j)),
                  pl.BlockSpec((tm, tn), lambda i, j: (i, j)),
                  pl.BlockSpec((1, N_MOD, tn), lambda i, j: ((i * tm) // seq, 0, j))],
        out_specs=pl.BlockSpec((tm, tn), lambda i, j: (i, j)),
        out_shape=jax.ShapeDtypeStruct((t, d), F32),
        compiler_params=_cparams(("parallel", "parallel")),
        name="outproj",
    )(oa, od, w_out, w_out, x2d, mod)


def _layer(layer, x, mod, positions, ffn1_norm, ffn1_w_gate, ffn1_w_up, ffn1_w_down, mix_norm, w_in, conv_w,
           q_norm, k_norm, a_log, dt_bias, delta_out_norm, w_out, ffn2_norm, ffn2_w_gate, ffn2_w_up,
           ffn2_w_down):
    b, seq, d = x.shape
    t = b * seq
    x2d = x.reshape(t, d)
    row = lambda v: v.reshape(1, -1).astype(F32)

    x2d = _ffn(x2d, mod, row(ffn1_norm), ffn1_w_gate, ffn1_w_up, ffn1_w_down, layer, sub=0, seq=seq)

    n_ab = w_in.shape[2] - MAIN_PROJ
    w_ab = jnp.pad(w_in[layer, :, MAIN_PROJ:], ((0, 0), (0, LANES - n_ab))).astype(BF16)
    proj, ab = _inproj(x2d, mod, row(mix_norm), jnp.swapaxes(w_in, 1, 2), w_ab, layer, seq=seq)
    proj = proj.reshape(b, seq, MAIN_PROJ)
    ab = ab.reshape(b, seq, LANES)

    lane = jnp.arange(HEAD_DIM)
    inv_freq = ROPE_THETA ** (-(lane % ROPE_HALF).astype(F32) / ROPE_HALF)
    invf = jnp.where(lane < 2 * ROPE_HALF, inv_freq, 0.0).reshape(1, HEAD_DIM).astype(F32)
    sgn = jnp.where(lane < ROPE_HALF, -1.0, jnp.where(lane < 2 * ROPE_HALF, 1.0, 0.0)).reshape(1, HEAD_DIM)
    pos_b = jnp.broadcast_to(positions.astype(F32)[..., None], (b, seq, HEAD_DIM))
    oa = _attention(proj, pos_b, invf, sgn.astype(F32), row(q_norm), row(k_norm))

    lane_vec = lambda v: jnp.pad(v.astype(F32), (0, LANES - N_DELTA_HEADS)).reshape(1, LANES)
    sel, selr = _selectors()
    u, w, qd, kt, intra, dec = _delta_prep(proj, ab, lane_vec(a_log), lane_vec(dt_bias), sel, selr,
                                           conv_w.astype(F32))
    od = _delta_scan(u, w, qd, kt, intra, dec, proj, row(delta_out_norm))

    x2d = _outproj(oa.reshape(t, ATTN_WIDTH), od.reshape(t, DELTA_WIDTH), w_out, x2d, mod, layer, seq=seq)

    x2d = _ffn(x2d, mod, row(ffn2_norm), ffn2_w_gate, ffn2_w_up, ffn2_w_down, layer, sub=2, seq=seq)
    return x2d.reshape(b, seq, d)


def kernel(x, c, positions, w_ada, b_ada, ffn1_norm, ffn1_w_gate, ffn1_w_up, ffn1_w_down, mix_norm, w_in,
           conv_w, q_norm, k_norm, a_log, dt_bias, delta_out_norm, w_out, ffn2_norm, ffn2_w_gate, ffn2_w_up,
           ffn2_w_down):
    b, _, d = x.shape
    depth = w_ada.shape[0]
    c8 = jnp.pad(c.astype(F32), ((0, SUBLANES - b), (0, 0)))
    for l in range(depth):
        mod = _adaln(c8, w_ada[l], b_ada[l].reshape(1, -1))[:b].reshape(b, N_MOD, d)
        x = _layer(l, x, mod, positions, ffn1_norm[l], ffn1_w_gate, ffn1_w_up, ffn1_w_down,
                   mix_norm[l], w_in, conv_w[l], q_norm[l], k_norm[l], a_log[l], dt_bias[l],
                   delta_out_norm[l], w_out, ffn2_norm[l], ffn2_w_gate, ffn2_w_up, ffn2_w_down)
    return x
```

```python
import functools

import jax
import jax.numpy as jnp
from jax import lax
from jax.experimental import pallas as pl
from jax.experimental.pallas import tpu as pltpu

F32 = jnp.float32
BF16 = jnp.bfloat16

HEAD_DIM = 128
N_ATTN_HEADS = 8
N_DELTA_HEADS = 8
ATTN_WIDTH = N_ATTN_HEADS * HEAD_DIM
DELTA_WIDTH = N_DELTA_HEADS * HEAD_DIM
MIX_WIDTH = ATTN_WIDTH + DELTA_WIDTH
DILATIONS = (1, 4, 16)
Q_BLOCK = 128
ATTN_UNROLL = 8
ATTN_ROWS = 512
ROPE_THETA = 500000.0
ROPE_HALF = HEAD_DIM // 8
CONV_WIDTH = 4
CHUNK = 64
PAIR = 2 * CHUNK
NORM_EPS = 1e-6
N_MOD = 9
MAIN_PROJ = 3 * ATTN_WIDTH + 4 * DELTA_WIDTH
LANES = 128
SUBLANES = 8
VMEM_LIMIT = 60 * 1024 * 1024
NEG_INF = float("-inf")
NT_DIMS = (((1,), (1,)), ((), ()))
TN_DIMS = (((0,), (0,)), ((), ()))

assert PAIR == LANES


def _cparams(semantics):
    return pltpu.CompilerParams(dimension_semantics=semantics, vmem_limit_bytes=VMEM_LIMIT)


def _pick(n, candidates):
    for c in candidates:
        if n % c == 0:
            return c
    return n


def _split_bf16(x, terms):
    parts = []
    for _ in range(terms):
        p = x.astype(BF16)
        parts.append(p)
        x = x - p.astype(F32)
    return parts


def _adaln_kernel(c_ref, w_ref, b_ref, o_ref):
    c = c_ref[...]
    ca = (c * jax.nn.sigmoid(c)).astype(BF16)
    o_ref[...] = jnp.dot(ca, w_ref[...].astype(BF16), preferred_element_type=F32) + b_ref[...]


def _adaln(c8, w, b):
    d, n = w.shape
    tn = _pick(n, (1024, 512, 256, 128))
    return pl.pallas_call(
        _adaln_kernel,
        grid=(n // tn,),
        in_specs=[pl.BlockSpec((SUBLANES, d), lambda j: (0, 0)),
                  pl.BlockSpec((d, tn), lambda j: (0, j)),
                  pl.BlockSpec((1, tn), lambda j: (0, j))],
        out_specs=pl.BlockSpec((SUBLANES, tn), lambda j: (0, j)),
        out_shape=jax.ShapeDtypeStruct((SUBLANES, n), F32),
        compiler_params=_cparams(("parallel",)),
        name="adaln",
    )(c8, w, b)


def _modulated(x, gain, mod_ref, sub):
    ms = jnp.mean(x * x, axis=-1, keepdims=True)
    y = x * lax.rsqrt(ms + NORM_EPS) * gain
    return y * (1.0 + mod_ref[0, 3 * sub + 1:3 * sub + 2, :]) + mod_ref[0, 3 * sub:3 * sub + 1, :]


def _ffn_kernel(x_ref, mod_ref, gain_ref, wg_ref, wu_ref, wd_ref, o_ref, h_ref, *, sub, nf, nsplit):
    f = pl.program_id(1)

    @pl.when(f == 0)
    def _():
        h_ref[...] = _modulated(x_ref[...], gain_ref[...], mod_ref, sub).astype(BF16)
        o_ref[...] = jnp.zeros_like(o_ref)

    h = h_ref[...]
    g = jnp.dot(h, wg_ref[...].astype(BF16), preferred_element_type=F32)
    u = jnp.dot(h, wu_ref[...].astype(BF16), preferred_element_type=F32)
    a = (g * jax.nn.sigmoid(g) * u).astype(BF16)
    wd = wd_ref[...].astype(BF16)
    dn = o_ref.shape[1] // nsplit
    for n in range(nsplit):
        cs = slice(n * dn, (n + 1) * dn)
        o_ref[:, cs] += jnp.dot(a, wd[:, cs], preferred_element_type=F32)

    @pl.when(f == nf - 1)
    def _():
        gate = 0.5 * mod_ref[0, 3 * sub + 2:3 * sub + 3, :]
        o_ref[...] = x_ref[...] + gate * o_ref[...]


def _ffn(x2d, mod, gain, wg, wu, wd, layer, *, sub, seq):
    t, d = x2d.shape
    dff = wg.shape[2]
    tm = _pick(seq, (1024, 512, 256, 128))
    tf = _pick(dff, (256, 128))
    nf = dff // tf
    return pl.pallas_call(
        functools.partial(_ffn_kernel, sub=sub, nf=nf, nsplit=d // _pick(d, (512, 256, 128))),
        grid=(t // tm, nf),
        in_specs=[pl.BlockSpec((tm, d), lambda i, f: (i, 0)),
                  pl.BlockSpec((1, N_MOD, d), lambda i, f: ((i * tm) // seq, 0, 0)),
                  pl.BlockSpec((1, d), lambda i, f: (0, 0)),
                  pl.BlockSpec((None, d, tf), lambda i, f: (layer, 0, f)),
                  pl.BlockSpec((None, d, tf), lambda i, f: (layer, 0, f)),
                  pl.BlockSpec((None, tf, d), lambda i, f: (layer, f, 0))],
        out_specs=pl.BlockSpec((tm, d), lambda i, f: (i, 0)),
        out_shape=jax.ShapeDtypeStruct((t, d), F32),
        scratch_shapes=[pltpu.VMEM((tm, d), BF16)],
        compiler_params=_cparams(("parallel", "arbitrary")),
        name=f"ffn{sub}",
    )(x2d, mod, gain, wg, wu, wd)


def _inproj_kernel(x_ref, mod_ref, gain_ref, w_ref, wab_ref, o_ref, ab_ref, h_ref):
    n = pl.program_id(1)

    @pl.when(n == 0)
    def _():
        h = _modulated(x_ref[...], gain_ref[...], mod_ref, 1).astype(BF16)
        h_ref[...] = h
        ab_ref[...] = jnp.dot(h, wab_ref[...], preferred_element_type=F32)

    o_ref[...] = lax.dot_general(h_ref[...], w_ref[...].astype(BF16), NT_DIMS, preferred_element_type=F32)


def _inproj(x2d, mod, gain, w_in_t, w_ab, layer, *, seq):
    t, d = x2d.shape
    n = MAIN_PROJ
    tm = _pick(seq, (1024, 512, 256, 128))
    tn = _pick(n, (1024, 512, 256, 128))
    return pl.pallas_call(
        _inproj_kernel,
        grid=(t // tm, n // tn),
        in_specs=[pl.BlockSpec((tm, d), lambda i, j: (i, 0)),
                  pl.BlockSpec((1, N_MOD, d), lambda i, j: ((i * tm) // seq, 0, 0)),
                  pl.BlockSpec((1, d), lambda i, j: (0, 0)),
                  pl.BlockSpec((None, tn, d), lambda i, j: (layer, j, 0)),
                  pl.BlockSpec((d, LANES), lambda i, j: (0, 0))],
        out_specs=[pl.BlockSpec((tm, tn), lambda i, j: (i, j)),
                   pl.BlockSpec((tm, LANES), lambda i, j: (i, 0))],
        out_shape=[jax.ShapeDtypeStruct((t, n), F32), jax.ShapeDtypeStruct((t, LANES), F32)],
        scratch_shapes=[pltpu.VMEM((tm, d), BF16)],
        compiler_params=_cparams(("parallel", "arbitrary")),
        name="inproj",
    )(x2d, mod, gain, w_in_t, w_ab)


def _attn_kernel(pos_ref, invf_ref, sgn_ref, qg_ref, kg_ref, q_ref, k_ref, v_ref, o_ref,
                 cos_ref, sin_ref, src_ref, d_ref, ob_ref, lb_ref, bias_ref, *, seq):
    h = pl.program_id(1)
    nb = len(DILATIONS)
    rc = ATTN_ROWS

    @pl.when(h == 0)
    def _():
        ang = pos_ref[0] * invf_ref[...]
        cos_ref[...] = jnp.cos(ang)
        sin_ref[...] = jnp.sin(ang) * sgn_ref[...]

    lane = lax.broadcasted_iota(jnp.int32, (rc, HEAD_DIM), 1)

    def norm_rope(x, gain, cs, sn):
        y = x * lax.rsqrt(jnp.mean(x * x, axis=-1, keepdims=True) + NORM_EPS) * gain
        partner = jnp.where(lane < ROPE_HALF, pltpu.roll(y, HEAD_DIM - ROPE_HALF, 1),
                            pltpu.roll(y, ROPE_HALF, 1))
        return y * cs + partner * sn

    def prep(c, carry):
        rs = pl.ds(pl.multiple_of(c * rc, rc), rc)
        cs, sn = cos_ref[rs, :], sin_ref[rs, :]
        src_ref[0, rs, :] = norm_rope(q_ref[0, rs, :], qg_ref[...], cs, sn) * (HEAD_DIM ** -0.5)
        src_ref[1, rs, :] = norm_rope(k_ref[0, rs, :], kg_ref[...], cs, sn)
        return carry

    lax.fori_loop(0, seq // rc, prep, 0)

    assert DILATIONS == (1, 4, 16)
    l4, l16 = seq // 4, seq // 16
    base = lambda p, t: (3 * p + t) * seq
    f4_ref = lb_ref
    for t in range(3):
        src = (lambda sl, t=t: src_ref[t, sl, :]) if t < 2 else (lambda sl: v_ref[0, sl, :])
        d_ref[base(0, t):base(0, t) + seq, :] = src(slice(0, seq)).astype(BF16)
        for r in range(4):
            x = src(pl.ds(r, l4, stride=4))
            f4_ref[t * seq + r * l4:t * seq + (r + 1) * l4, :] = x
            d_ref[base(1, t) + r * l4:base(1, t) + (r + 1) * l4, :] = x.astype(BF16)
        for r4 in range(4):
            for r2 in range(4):
                res = r4 + 4 * r2
                x = f4_ref[pl.ds(t * seq + r4 * l4 + r2, l16, stride=4), :]
                d_ref[base(2, t) + res * l16:base(2, t) + (res + 1) * l16, :] = x.astype(BF16)

    qi = lax.broadcasted_iota(jnp.int32, (Q_BLOCK, 2 * Q_BLOCK), 0)
    kj = lax.broadcasted_iota(jnp.int32, (Q_BLOCK, 2 * Q_BLOCK), 1)
    dist = qi + Q_BLOCK - kj
    bias_ref[0] = jnp.where((dist >= 0) & (dist <= Q_BLOCK), 0.0, NEG_INF)
    bias_ref[1] = jnp.where(kj <= qi, 0.0, NEG_INF)

    nblocks = seq // Q_BLOCK
    for p, d in enumerate(DILATIONS):
        nblk = nblocks // d
        assert nblk >= 2

        def rows(start, size, d=d):
            return pl.ds(start, size) if d == 1 else pl.ds(start, size, stride=d)

        def group(it, carry, d=d, nblk=nblk, p=p, rows=rows):
            qrow, krow, tok, first = [], [], [], []
            for j in range(ATTN_UNROLL):
                t = it * ATTN_UNROLL + j
                r = lax.div(t, nblk)
                n = lax.rem(t, nblk)
                qrow.append(pl.multiple_of((r * nblk + n) * Q_BLOCK, Q_BLOCK))
                krow.append(pl.multiple_of((r * nblk + jnp.maximum(n - 1, 0)) * Q_BLOCK, Q_BLOCK))
                tok.append(r + d * Q_BLOCK * n)
                first.append(jnp.where(n == 0, 1, 0))
            q = [d_ref[pl.ds(base(p, 0) + x, Q_BLOCK), :] for x in qrow]
            k = [d_ref[pl.ds(base(p, 1) + x, 2 * Q_BLOCK), :] for x in krow]
            v = [d_ref[pl.ds(base(p, 2) + x, 2 * Q_BLOCK), :] for x in krow]
            s = [lax.dot_general(q[j], k[j], NT_DIMS, preferred_element_type=F32) + bias_ref[first[j]]
                 for j in range(ATTN_UNROLL)]
            m = [jnp.max(x, axis=-1, keepdims=True) for x in s]
            e = [jnp.exp(s[j] - m[j]) for j in range(ATTN_UNROLL)]
            l = [jnp.sum(x, axis=-1, keepdims=True) for x in e]
            acc = [jnp.dot(e[j].astype(BF16), v[j], preferred_element_type=F32) for j in range(ATTN_UNROLL)]
            for j in range(ATTN_UNROLL):
                ob_ref[rows(p * seq + tok[j], Q_BLOCK), :] = acc[j] / l[j]
                lb_ref[rows(p * seq + tok[j], Q_BLOCK), :] = jnp.broadcast_to(
                    m[j] + jnp.log(l[j]), (Q_BLOCK, HEAD_DIM))
            return carry

        lax.fori_loop(0, nblocks // ATTN_UNROLL, group, 0)

    def mix(c, carry):
        r0 = pl.multiple_of(c * rc, rc)
        lses = [lb_ref[pl.ds(p * seq + r0, rc), :] for p in range(nb)]
        mx = functools.reduce(jnp.maximum, lses)
        ws = [jnp.exp(x - mx) for x in lses]
        num = functools.reduce(lambda a, b: a + b,
                               [w * ob_ref[pl.ds(p * seq + r0, rc), :] for p, w in enumerate(ws)])
        den = functools.reduce(lambda a, b: a + b, ws)
        o_ref[0, pl.ds(r0, rc), :] = (num / den).astype(o_ref.dtype)
        return carry

    lax.fori_loop(0, seq // rc, mix, 0)


def _attention(proj, pos_b, invf, sgn, qg, kg):
    b, seq, _ = proj.shape
    ha = N_ATTN_HEADS
    blk = lambda off: pl.BlockSpec((1, seq, HEAD_DIM), lambda i, h: (i, 0, off + h))
    vec = pl.BlockSpec((1, HEAD_DIM), lambda i, h: (0, 0))
    nd = len(DILATIONS)
    assert seq % (ATTN_UNROLL * Q_BLOCK) == 0 and seq % ATTN_ROWS == 0
    return pl.pallas_call(
        functools.partial(_attn_kernel, seq=seq),
        grid=(b, ha),
        in_specs=[pl.BlockSpec((1, seq, HEAD_DIM), lambda i, h: (i, 0, 0)),
                  vec, vec, vec, vec, blk(0), blk(ha), blk(2 * ha)],
        out_specs=pl.BlockSpec((1, seq, HEAD_DIM), lambda i, h: (i, 0, h)),
        out_shape=jax.ShapeDtypeStruct((b, seq, ATTN_WIDTH), BF16),
        scratch_shapes=[pltpu.VMEM((seq, HEAD_DIM), F32),
                        pltpu.VMEM((seq, HEAD_DIM), F32),
                        pltpu.VMEM((2, seq, HEAD_DIM), F32),
                        pltpu.VMEM((3 * nd * seq, HEAD_DIM), BF16),
                        pltpu.VMEM((nd * seq, HEAD_DIM), F32),
                        pltpu.VMEM((nd * seq, HEAD_DIM), F32),
                        pltpu.VMEM((2, Q_BLOCK, 2 * Q_BLOCK), F32)],
        compiler_params=_cparams(("parallel", "arbitrary")),
        name="dilated_attn",
    )(pos_b, invf, sgn, qg, kg, proj, proj, proj)


def _softplus(x):
    return jnp.maximum(x, 0.0) + jnp.log1p(jnp.exp(-jnp.abs(x)))


def _delta_prep_kernel(ab_ref, alog_ref, dtb_ref, sel_ref, selr_ref, cwq_ref, cwk_ref, cwv_ref,
                       q_ref, k_ref, v_ref, hq_ref, hk_ref, hv_ref,
                       u_ref, w_ref, qd_ref, kt_ref, in_ref, dec_ref,
                       xp_ref, comp_ref, *, ts):
    i = pl.program_id(1)
    h = pl.program_id(2)
    npair = ts // PAIR
    ci = lax.broadcasted_iota(jnp.int32, (PAIR, PAIR), 0)
    cj = lax.broadcasted_iota(jnp.int32, (PAIR, PAIR), 1)
    same = (ci // CHUNK) == (cj // CHUNK)
    causal = same & (ci >= cj)
    strict = same & (ci > cj)
    eye = (ci == cj).astype(F32)

    @pl.when(h == 0)
    def _():
        lane = lax.broadcasted_iota(jnp.int32, (PAIR, LANES), 1)
        tril = causal.astype(F32)
        for p in range(npair):
            rs = slice(p * PAIR, (p + 1) * PAIR)
            ab = ab_ref[0, rs, :]
            g = -jnp.exp(alog_ref[...]) * _softplus(ab + dtb_ref[...])
            gc = jnp.dot(tril, g, preferred_element_type=F32, precision=lax.Precision.HIGHEST)
            comp = jnp.where(lane < N_DELTA_HEADS, gc, jax.nn.sigmoid(ab))
            for t, part in enumerate(_split_bf16(comp, 3)):
                comp_ref[rs, t * LANES:(t + 1) * LANES] = part

    def conv_silu(x_ref, halo_ref, cw_ref):
        halo = halo_ref[0]
        xp_ref[0:SUBLANES, :] = jnp.where(i > 0, halo, jnp.zeros_like(halo))
        xp_ref[SUBLANES:SUBLANES + ts, :] = x_ref[0]
        y = jnp.zeros((ts, HEAD_DIM), F32)
        for j in range(CONV_WIDTH):
            off = SUBLANES - (CONV_WIDTH - 1) + j
            y = y + cw_ref[j:j + 1, :] * xp_ref[off:off + ts, :]
        return y * jax.nn.sigmoid(y)

    ones2 = jnp.ones((2 * HEAD_DIM, HEAD_DIM), BF16)

    def l2n(x):
        ss = jnp.dot(jnp.concatenate(_split_bf16(x * x, 2), axis=1), ones2, preferred_element_type=F32)
        return x * lax.rsqrt(ss + NORM_EPS)

    q = l2n(conv_silu(q_ref, hq_ref, cwq_ref)) * (HEAD_DIM ** -0.5)
    k = l2n(conv_silu(k_ref, hk_ref, cwk_ref))
    v = conv_silu(v_ref, hv_ref, cwv_ref)
    rep = jnp.dot(comp_ref[...], sel_ref[0], preferred_element_type=F32)

    merge_masks = []
    s = 1
    while s < CHUNK:
        merge_masks.append(((ci // s) % 2 == 1) & ((cj // s) == (ci // s) - 1))
        s *= 2
    row = lax.broadcasted_iota(jnp.int32, (PAIR, LANES), 0)

    pairs = range(npair)
    rsl = [slice(p * PAIR, (p + 1) * PAIR) for p in pairs]
    gc = [rep[rs, :LANES] for rs in rsl]
    beta = [rep[rs, LANES:] for rs in rsl]
    grow = [lax.dot_general(selr_ref[0], comp_ref[rs, :], NT_DIMS, preferred_element_type=F32)
            for rs in rsl]
    gamma = [jnp.exp(jnp.where(causal, gc[p] - grow[p], NEG_INF)) for p in pairs]
    kb = [k[rsl[p]] * beta[p] for p in pairs]
    kbf = [k[rs].astype(BF16) for rs in rsl]
    kk = [lax.dot_general(kb[p].astype(BF16), kbf[p], NT_DIMS, preferred_element_type=F32) for p in pairs]
    a = [jnp.where(strict, kk[p] * gamma[p], 0.0) for p in pairs]
    t = [eye - jnp.where(merge_masks[0], a[p], 0.0) for p in pairs]
    for off in merge_masks[1:]:
        tb = [t[p].astype(BF16) for p in pairs]
        x = [jnp.dot(jnp.where(off, a[p], 0.0).astype(BF16), tb[p], preferred_element_type=F32).astype(BF16)
             for p in pairs]
        t = [t[p] - jnp.dot(tb[p], x[p], preferred_element_type=F32) for p in pairs]
    eg = [jnp.exp(gc[p]) for p in pairs]
    uw = [jnp.dot(t[p].astype(BF16),
                  jnp.concatenate([v[rsl[p]] * beta[p], kb[p] * eg[p]], axis=1).astype(BF16),
                  preferred_element_type=F32) for p in pairs]
    for p in pairs:
        rs = rsl[p]
        intra = lax.dot_general(q[rs].astype(BF16), kbf[p], NT_DIMS, preferred_element_type=F32) * gamma[p]
        g0 = gc[p][CHUNK - 1:CHUNK, :]
        g1 = gc[p][PAIR - 1:PAIR, :]
        glast = jnp.where(row < CHUNK, g0, g1)
        u_ref[0, rs, :] = uw[p][:, :HEAD_DIM]
        w_ref[0, rs, :] = uw[p][:, HEAD_DIM:].astype(BF16)
        qd_ref[0, rs, :] = (q[rs] * eg[p]).astype(BF16)
        kt_ref[0, rs, :] = (k[rs] * jnp.exp(glast - gc[p])).astype(BF16)
        in_ref[0, rs, :] = intra.astype(BF16)
        dec_ref[0, 0, 2 * p:2 * p + 1, :] = jnp.exp(g0)
        dec_ref[0, 0, 2 * p + 1:2 * p + 2, :] = jnp.exp(g1)


def _delta_prep(proj, ab, alog_vec, dtb_vec, sel, selr, conv_w):
    b, seq, _ = proj.shape
    hd = N_DELTA_HEADS
    ts = _pick(seq, (1024, 512, 256, 128))
    q0 = 3 * N_ATTN_HEADS
    tile = lambda off: pl.BlockSpec((1, ts, HEAD_DIM), lambda bi, i, h: (bi, i, off + h))
    halo = lambda off: pl.BlockSpec(
        (1, SUBLANES, HEAD_DIM),
        lambda bi, i, h: (bi, jnp.maximum(i * (ts // SUBLANES) - 1, 0), off + h))
    cw = lambda off: pl.BlockSpec((CONV_WIDTH, HEAD_DIM), lambda bi, i, h: (0, off + h))
    vec = pl.BlockSpec((1, LANES), lambda bi, i, h: (0, 0))
    out_tile = pl.BlockSpec((1, ts, HEAD_DIM), lambda bi, i, h: (bi, i, h))
    big = lambda dt: jax.ShapeDtypeStruct((b, seq, DELTA_WIDTH), dt)
    return pl.pallas_call(
        functools.partial(_delta_prep_kernel, ts=ts),
        grid=(b, seq // ts, hd),
        in_specs=[pl.BlockSpec((1, ts, LANES), lambda bi, i, h: (bi, i, 0)),
                  vec, vec,
                  pl.BlockSpec((1, 3 * LANES, 2 * LANES), lambda bi, i, h: (h, 0, 0)),
                  pl.BlockSpec((1, PAIR, 3 * LANES), lambda bi, i, h: (h, 0, 0)),
                  cw(0), cw(hd), cw(2 * hd),
                  tile(q0), tile(q0 + hd), tile(q0 + 2 * hd),
                  halo(q0), halo(q0 + hd), halo(q0 + 2 * hd)],
        out_specs=[out_tile, out_tile, out_tile, out_tile, out_tile,
                   pl.BlockSpec((1, 1, ts // CHUNK, HEAD_DIM), lambda bi, i, h: (bi, h, i, 0))],
        out_shape=[big(F32), big(BF16), big(BF16), big(BF16), big(BF16),
                   jax.ShapeDtypeStruct((b, hd, seq // CHUNK, HEAD_DIM), F32)],
        scratch_shapes=[pltpu.VMEM((SUBLANES + ts, HEAD_DIM), F32),
                        pltpu.VMEM((ts, 3 * LANES), BF16)],
        compiler_params=_cparams(("parallel", "parallel", "arbitrary")),
        name="delta_prep",
    )(ab, alog_vec, dtb_vec, sel, selr, conv_w, conv_w, conv_w, proj, proj, proj, proj, proj, proj)


def _selectors():
    hd = N_DELTA_HEADS
    src = jnp.arange(3 * LANES) % LANES
    dst = jnp.arange(2 * LANES) // LANES
    head = jnp.arange(hd)[:, None, None]
    sel = (src[None, :, None] == head + hd * dst[None, None, :]).astype(BF16)
    selr = jnp.broadcast_to((src[None, None, :] == head).astype(BF16), (hd, PAIR, 3 * LANES))
    return sel, selr


def _delta_scan_kernel(u_ref, w_ref, qd_ref, kt_ref, in_ref, dec_ref, z_ref, gain_ref, o_ref, s_ref,
                       *, nb, npair):
    @pl.when(pl.program_id(0) == 0)
    def _():
        s_ref[...] = jnp.zeros_like(s_ref)

    gain = gain_ref[...]

    chains = [(bi, h) for bi in range(nb) for h in range(N_DELTA_HEADS)]
    cols = lambda h: slice(h * HEAD_DIM, (h + 1) * HEAD_DIM)

    def pair(pi, carry):
        for e in range(2):
            sl = pl.ds(pl.multiple_of(pi * PAIR + e * CHUNK, CHUNK), CHUNK)
            st = [s_ref[bi * N_DELTA_HEADS + h] for bi, h in chains]
            sb = [s.astype(BF16) for s in st]
            ws = [jnp.dot(w_ref[bi, sl, cols(h)], sb[n], preferred_element_type=F32)
                  for n, (bi, h) in enumerate(chains)]
            qs = [jnp.dot(qd_ref[bi, sl, cols(h)], sb[n], preferred_element_type=F32)
                  for n, (bi, h) in enumerate(chains)]
            vb = [(u_ref[bi, sl, cols(h)] - ws[n]).astype(BF16) for n, (bi, h) in enumerate(chains)]
            zero = jnp.zeros((CHUNK, HEAD_DIM), BF16)
            vpair = [jnp.concatenate([x, zero] if e == 0 else [zero, x], axis=0) for x in vb]
            o = [qs[n] + jnp.dot(in_ref[bi, sl, cols(h)], vpair[n], preferred_element_type=F32)
                 for n, (bi, h) in enumerate(chains)]
            for n, (bi, h) in enumerate(chains):
                dec = dec_ref[bi, h, pl.ds(2 * pi + e, 1), :]
                s_ref[bi * N_DELTA_HEADS + h] = st[n] * dec + lax.dot_general(
                    kt_ref[bi, sl, cols(h)], vb[n], TN_DIMS, preferred_element_type=F32)
            for n, (bi, h) in enumerate(chains):
                y = o[n] * lax.rsqrt(jnp.mean(o[n] * o[n], axis=-1, keepdims=True) + NORM_EPS) * gain
                z = z_ref[bi, sl, cols(h)]
                o_ref[bi, sl, cols(h)] = (y * (z * jax.nn.sigmoid(z))).astype(o_ref.dtype)
        return carry

    lax.fori_loop(0, npair, pair, 0)


def _delta_scan(u, w, qd, kt, intra, dec, proj, gain):
    b, seq, _ = u.shape
    cg = SUBLANES
    rows = cg * CHUNK
    zblk = (3 * ATTN_WIDTH + 3 * DELTA_WIDTH) // DELTA_WIDTH
    tile = pl.BlockSpec((b, rows, DELTA_WIDTH), lambda i: (0, i, 0))
    return pl.pallas_call(
        functools.partial(_delta_scan_kernel, nb=b, npair=rows // PAIR),
        grid=(seq // rows,),
        in_specs=[tile, tile, tile, tile, tile,
                  pl.BlockSpec((b, N_DELTA_HEADS, cg, HEAD_DIM), lambda i: (0, 0, i, 0)),
                  pl.BlockSpec((b, rows, DELTA_WIDTH), lambda i: (0, i, zblk)),
                  pl.BlockSpec((1, HEAD_DIM), lambda i: (0, 0))],
        out_specs=tile,
        out_shape=jax.ShapeDtypeStruct((b, seq, DELTA_WIDTH), BF16),
        scratch_shapes=[pltpu.VMEM((b * N_DELTA_HEADS, HEAD_DIM, HEAD_DIM), F32)],
        compiler_params=_cparams(("arbitrary",)),
        name="delta_scan",
    )(u, w, qd, kt, intra, dec, proj, gain)


def _outproj_kernel(oa_ref, od_ref, wa_ref, wd_ref, x_ref, mod_ref, o_ref):
    y = (jnp.dot(oa_ref[...], wa_ref[...].astype(BF16), preferred_element_type=F32)
         + jnp.dot(od_ref[...], wd_ref[...].astype(BF16), preferred_element_type=F32))
    o_ref[...] = x_ref[...] + mod_ref[0, 5:6, :] * y


def _outproj(oa, od, w_out, x2d, mod, layer, *, seq):
    t, d = x2d.shape
    tm = _pick(seq, (1024, 512, 256, 128))
    tn = _pick(d, (1024, 512, 256, 128))
    assert ATTN_WIDTH == DELTA_WIDTH
    return pl.pallas_call(
        _outproj_kernel,
        grid=(d // tn, t // tm),
        in_specs=[pl.BlockSpec((tm, ATTN_WIDTH), lambda j, i: (i, 0)),
                  pl.BlockSpec((tm, DELTA_WIDTH), lambda j, i: (i, 0)),
                  pl.BlockSpec((None, ATTN_WIDTH, tn), lambda j, i: (layer, 0, j)),
                  pl.BlockSpec((None, DELTA_WIDTH, tn), lambda j, i: (layer, 1, j)),
                  pl.BlockSpec((tm, tn), lambda j, i: (i, j)),
                  pl.BlockSpec((1, N_MOD, tn), lambda j, i: ((i * tm) // seq, 0, j))],
        out_specs=pl.BlockSpec((tm, tn), lambda j, i: (i, j)),
        out_shape=jax.ShapeDtypeStruct((t, d), F32),
        compiler_params=_cparams(("parallel", "parallel")),
        name="outproj",
    )(oa, od, w_out, w_out, x2d, mod)


def _layer(layer, x, mod, positions, ffn1_norm, ffn1_w_gate, ffn1_w_up, ffn1_w_down, mix_norm, w_in, conv_w,
           q_norm, k_norm, a_log, dt_bias, delta_out_norm, w_out, ffn2_norm, ffn2_w_gate, ffn2_w_up,
           ffn2_w_down):
    b, seq, d = x.shape
    t = b * seq
    x2d = x.reshape(t, d)
    row = lambda v: v.reshape(1, -1).astype(F32)

    x2d = _ffn(x2d, mod, row(ffn1_norm), ffn1_w_gate, ffn1_w_up, ffn1_w_down, layer, sub=0, seq=seq)

    n_ab = w_in.shape[2] - MAIN_PROJ
    w_ab = jnp.pad(w_in[layer, :, MAIN_PROJ:], ((0, 0), (0, LANES - n_ab))).astype(BF16)
    proj, ab = _inproj(x2d, mod, row(mix_norm), jnp.swapaxes(w_in, 1, 2), w_ab, layer, seq=seq)
    proj = proj.reshape(b, seq, MAIN_PROJ)
    ab = ab.reshape(b, seq, LANES)

    lane = jnp.arange(HEAD_DIM)
    inv_freq = ROPE_THETA ** (-(lane % ROPE_HALF).astype(F32) / ROPE_HALF)
    invf = jnp.where(lane < 2 * ROPE_HALF, inv_freq, 0.0).reshape(1, HEAD_DIM).astype(F32)
    sgn = jnp.where(lane < ROPE_HALF, -1.0, jnp.where(lane < 2 * ROPE_HALF, 1.0, 0.0)).reshape(1, HEAD_DIM)
    pos_b = jnp.broadcast_to(positions.astype(F32)[..., None], (b, seq, HEAD_DIM))
    oa = _attention(proj, pos_b, invf, sgn.astype(F32), row(q_norm), row(k_norm))

    lane_vec = lambda v: jnp.pad(v.astype(F32), (0, LANES - N_DELTA_HEADS)).reshape(1, LANES)
    sel, selr = _selectors()
    u, w, qd, kt, intra, dec = _delta_prep(proj, ab, lane_vec(a_log), lane_vec(dt_bias), sel, selr,
                                           conv_w.astype(F32))
    od = _delta_scan(u, w, qd, kt, intra, dec, proj, row(delta_out_norm))

    x2d = _outproj(oa.reshape(t, ATTN_WIDTH), od.reshape(t, DELTA_WIDTH), w_out, x2d, mod, layer, seq=seq)

    x2d = _ffn(x2d, mod, row(ffn2_norm), ffn2_w_gate, ffn2_w_up, ffn2_w_down, layer, sub=2, seq=seq)
    return x2d.reshape(b, seq, d)


def kernel(x, c, positions, w_ada, b_ada, ffn1_norm, ffn1_w_gate, ffn1_w_up, ffn1_w_down, mix_norm, w_in,
           conv_w, q_norm, k_norm, a_log, dt_bias, delta_out_norm, w_out, ffn2_norm, ffn2_w_gate, ffn2_w_up,
           ffn2_w_down):
    b, _, d = x.shape
    depth = w_ada.shape[0]
    c8 = jnp.pad(c.astype(F32), ((0, SUBLANES - b), (0, 0)))
    for l in range(depth):
        mod = _adaln(c8, w_ada[l], b_ada[l].reshape(1, -1))[:b].reshape(b, N_MOD, d)
        x = _layer(l, x, mod, positions, ffn1_norm[l], ffn1_w_gate, ffn1_w_up, ffn1_w_down,
                   mix_norm[l], w_in, conv_w[l], q_norm[l], k_norm[l], a_log[l], dt_bias[l],
                   delta_out_norm[l], w_out, ffn2_norm[l], ffn2_w_gate, ffn2_w_up, ffn2_w_down)
    return x
```

```python
import functools

import jax
import jax.numpy as jnp
from jax import lax
from jax.experimental import pallas as pl
from jax.experimental.pallas import tpu as pltpu

F32 = jnp.float32
BF16 = jnp.bfloat16

HEAD_DIM = 128
N_ATTN_HEADS = 8
N_DELTA_HEADS = 8
ATTN_WIDTH = N_ATTN_HEADS * HEAD_DIM
DELTA_WIDTH = N_DELTA_HEADS * HEAD_DIM
MIX_WIDTH = ATTN_WIDTH + DELTA_WIDTH
DILATIONS = (1, 4, 16)
Q_BLOCK = 128
ATTN_UNROLL = 8
ATTN_ROWS = 512
ROPE_THETA = 500000.0
ROPE_HALF = HEAD_DIM // 8
CONV_WIDTH = 4
CHUNK = 64
PAIR = 2 * CHUNK
NORM_EPS = 1e-6
N_MOD = 9
MAIN_PROJ = 3 * ATTN_WIDTH + 4 * DELTA_WIDTH
LANES = 128
SUBLANES = 8
VMEM_LIMIT = 60 * 1024 * 1024
NEG_INF = float("-inf")
NT_DIMS = (((1,), (1,)), ((), ()))
TN_DIMS = (((0,), (0,)), ((), ()))

assert PAIR == LANES


def _cparams(semantics):
    return pltpu.CompilerParams(dimension_semantics=semantics, vmem_limit_bytes=VMEM_LIMIT)


def _pick(n, candidates):
    for c in candidates:
        if n % c == 0:
            return c
    return n


def _split_bf16(x, terms):
    parts = []
    for _ in range(terms):
        p = x.astype(BF16)
        parts.append(p)
        x = x - p.astype(F32)
    return parts


def _adaln_kernel(c_ref, w_ref, b_ref, o_ref):
    c = c_ref[...]
    ca = (c * jax.nn.sigmoid(c)).astype(BF16)
    o_ref[...] = jnp.dot(ca, w_ref[...].astype(BF16), preferred_element_type=F32) + b_ref[...]


def _adaln(c8, w, b):
    d, n = w.shape
    tn = _pick(n, (1024, 512, 256, 128))
    return pl.pallas_call(
        _adaln_kernel,
        grid=(n // tn,),
        in_specs=[pl.BlockSpec((SUBLANES, d), lambda j: (0, 0)),
                  pl.BlockSpec((d, tn), lambda j: (0, j)),
                  pl.BlockSpec((1, tn), lambda j: (0, j))],
        out_specs=pl.BlockSpec((SUBLANES, tn), lambda j: (0, j)),
        out_shape=jax.ShapeDtypeStruct((SUBLANES, n), F32),
        compiler_params=_cparams(("parallel",)),
        name="adaln",
    )(c8, w, b)


def _modulated(x, gain, mod_ref, sub):
    ms = jnp.mean(x * x, axis=-1, keepdims=True)
    y = x * lax.rsqrt(ms + NORM_EPS) * gain
    return y * (1.0 + mod_ref[0, 3 * sub + 1:3 * sub + 2, :]) + mod_ref[0, 3 * sub:3 * sub + 1, :]


def _ffn_kernel(x_ref, mod_ref, gain_ref, wg_ref, wu_ref, wd_ref, o_ref, h_ref, *, sub, nf, nsplit):
    f = pl.program_id(1)

    @pl.when(f == 0)
    def _():
        h_ref[...] = _modulated(x_ref[...], gain_ref[...], mod_ref, sub).astype(BF16)
        o_ref[...] = jnp.zeros_like(o_ref)

    h = h_ref[...]
    g = jnp.dot(h, wg_ref[...].astype(BF16), preferred_element_type=F32)
    u = jnp.dot(h, wu_ref[...].astype(BF16), preferred_element_type=F32)
    a = (g * jax.nn.sigmoid(g) * u).astype(BF16)
    wd = wd_ref[...].astype(BF16)
    dn = o_ref.shape[1] // nsplit
    for n in range(nsplit):
        cs = slice(n * dn, (n + 1) * dn)
        o_ref[:, cs] += jnp.dot(a, wd[:, cs], preferred_element_type=F32)

    @pl.when(f == nf - 1)
    def _():
        gate = 0.5 * mod_ref[0, 3 * sub + 2:3 * sub + 3, :]
        o_ref[...] = x_ref[...] + gate * o_ref[...]


def _ffn(x2d, mod, gain, wg, wu, wd, layer, *, sub, seq):
    t, d = x2d.shape
    dff = wg.shape[2]
    tm = _pick(seq, (1024, 512, 256, 128))
    tf = _pick(dff, (256, 128))
    nf = dff // tf
    return pl.pallas_call(
        functools.partial(_ffn_kernel, sub=sub, nf=nf, nsplit=d // _pick(d, (512, 256, 128))),
        grid=(t // tm, nf),
        in_specs=[pl.BlockSpec((tm, d), lambda i, f: (i, 0)),
                  pl.BlockSpec((1, N_MOD, d), lambda i, f: ((i * tm) // seq, 0, 0)),
                  pl.BlockSpec((1, d), lambda i, f: (0, 0)),
                  pl.BlockSpec((None, d, tf), lambda i, f: (layer, 0, f)),
                  pl.BlockSpec((None, d, tf), lambda i, f: (layer, 0, f)),
                  pl.BlockSpec((None, tf, d), lambda i, f: (layer, f, 0))],
        out_specs=pl.BlockSpec((tm, d), lambda i, f: (i, 0)),
        out_shape=jax.ShapeDtypeStruct((t, d), F32),
        scratch_shapes=[pltpu.VMEM((tm, d), BF16)],
        compiler_params=_cparams(("parallel", "arbitrary")),
        name=f"ffn{sub}",
    )(x2d, mod, gain, wg, wu, wd)


def _inproj_kernel(x_ref, mod_ref, gain_ref, w_ref, wab_ref, o_ref, ab_ref, h_ref):
    n = pl.program_id(1)

    @pl.when(n == 0)
    def _():
        h = _modulated(x_ref[...], gain_ref[...], mod_ref, 1).astype(BF16)
        h_ref[...] = h
        ab_ref[...] = jnp.dot(h, wab_ref[...], preferred_element_type=F32)

    o_ref[...] = lax.dot_general(h_ref[...], w_ref[...].astype(BF16), NT_DIMS, preferred_element_type=F32)


def _inproj(x2d, mod, gain, w_in_t, w_ab, layer, *, seq):
    t, d = x2d.shape
    n = MAIN_PROJ
    tm = _pick(seq, (1024, 512, 256, 128))
    tn = _pick(n, (1024, 512, 256, 128))
    return pl.pallas_call(
        _inproj_kernel,
        grid=(t // tm, n // tn),
        in_specs=[pl.BlockSpec((tm, d), lambda i, j: (i, 0)),
                  pl.BlockSpec((1, N_MOD, d), lambda i, j: ((i * tm) // seq, 0, 0)),
                  pl.BlockSpec((1, d), lambda i, j: (0, 0)),
                  pl.BlockSpec((None, tn, d), lambda i, j: (layer, j, 0)),
                  pl.BlockSpec((d, LANES), lambda i, j: (0, 0))],
        out_specs=[pl.BlockSpec((tm, tn), lambda i, j: (i, j)),
                   pl.BlockSpec((tm, LANES), lambda i, j: (i, 0))],
        out_shape=[jax.ShapeDtypeStruct((t, n), F32), jax.ShapeDtypeStruct((t, LANES), F32)],
        scratch_shapes=[pltpu.VMEM((tm, d), BF16)],
        compiler_params=_cparams(("parallel", "arbitrary")),
        name="inproj",
    )(x2d, mod, gain, w_in_t, w_ab)


def _attn_kernel(pos_ref, invf_ref, sgn_ref, qg_ref, kg_ref, q_ref, k_ref, v_ref, o_ref,
                 cos_ref, sin_ref, src_ref, d_ref, ob_ref, lb_ref, bias_ref, *, seq):
    h = pl.program_id(1)
    nb = len(DILATIONS)
    rc = ATTN_ROWS

    @pl.when(h == 0)
    def _():
        ang = pos_ref[0] * invf_ref[...]
        cos_ref[...] = jnp.cos(ang)
        sin_ref[...] = jnp.sin(ang) * sgn_ref[...]

    lane = lax.broadcasted_iota(jnp.int32, (rc, HEAD_DIM), 1)

    mean_w = jnp.full((2 * HEAD_DIM, HEAD_DIM), 1.0 / HEAD_DIM, BF16)

    def norm_rope(x, gain, cs, sn):
        ms = jnp.dot(jnp.concatenate(_split_bf16(x * x, 2), axis=1), mean_w, preferred_element_type=F32)
        y = x * lax.rsqrt(ms + NORM_EPS) * gain
        partner = jnp.where(lane < ROPE_HALF, pltpu.roll(y, HEAD_DIM - ROPE_HALF, 1),
                            pltpu.roll(y, ROPE_HALF, 1))
        return y * cs + partner * sn

    def prep(c, carry):
        rs = pl.ds(pl.multiple_of(c * rc, rc), rc)
        cs, sn = cos_ref[rs, :], sin_ref[rs, :]
        src_ref[0, rs, :] = norm_rope(q_ref[0, rs, :], qg_ref[...], cs, sn) * (HEAD_DIM ** -0.5)
        src_ref[1, rs, :] = norm_rope(k_ref[0, rs, :], kg_ref[...], cs, sn)
        return carry

    lax.fori_loop(0, seq // rc, prep, 0)

    assert DILATIONS == (1, 4, 16)
    l4, l16 = seq // 4, seq // 16
    base = lambda p, t: (3 * p + t) * seq
    f4_ref = lb_ref
    for t in range(3):
        src = (lambda sl, t=t: src_ref[t, sl, :]) if t < 2 else (lambda sl: v_ref[0, sl, :])
        d_ref[base(0, t):base(0, t) + seq, :] = src(slice(0, seq)).astype(BF16)
        for r in range(4):
            x = src(pl.ds(r, l4, stride=4))
            f4_ref[t * seq + r * l4:t * seq + (r + 1) * l4, :] = x
            d_ref[base(1, t) + r * l4:base(1, t) + (r + 1) * l4, :] = x.astype(BF16)
        for r4 in range(4):
            for r2 in range(4):
                res = r4 + 4 * r2
                x = f4_ref[pl.ds(t * seq + r4 * l4 + r2, l16, stride=4), :]
                d_ref[base(2, t) + res * l16:base(2, t) + (res + 1) * l16, :] = x.astype(BF16)

    qi = lax.broadcasted_iota(jnp.int32, (Q_BLOCK, 2 * Q_BLOCK), 0)
    kj = lax.broadcasted_iota(jnp.int32, (Q_BLOCK, 2 * Q_BLOCK), 1)
    dist = qi + Q_BLOCK - kj
    bias_ref[0] = jnp.where((dist >= 0) & (dist <= Q_BLOCK), 0.0, NEG_INF)
    bias_ref[1] = jnp.where(kj <= qi, 0.0, NEG_INF)

    nblocks = seq // Q_BLOCK
    for p, d in enumerate(DILATIONS):
        nblk = nblocks // d
        assert nblk >= 2

        def rows(start, size, d=d):
            return pl.ds(start, size) if d == 1 else pl.ds(start, size, stride=d)

        def group(it, carry, d=d, nblk=nblk, p=p, rows=rows):
            qrow, krow, tok, first = [], [], [], []
            for j in range(ATTN_UNROLL):
                t = it * ATTN_UNROLL + j
                r = lax.div(t, nblk)
                n = lax.rem(t, nblk)
                qrow.append(pl.multiple_of((r * nblk + n) * Q_BLOCK, Q_BLOCK))
                krow.append(pl.multiple_of((r * nblk + jnp.maximum(n - 1, 0)) * Q_BLOCK, Q_BLOCK))
                tok.append(r + d * Q_BLOCK * n)
                first.append(jnp.where(n == 0, 1, 0))
            q = [d_ref[pl.ds(base(p, 0) + x, Q_BLOCK), :] for x in qrow]
            k = [d_ref[pl.ds(base(p, 1) + x, 2 * Q_BLOCK), :] for x in krow]
            v = [d_ref[pl.ds(base(p, 2) + x, 2 * Q_BLOCK), :] for x in krow]
            s = [lax.dot_general(q[j], k[j], NT_DIMS, preferred_element_type=F32) + bias_ref[first[j]]
                 for j in range(ATTN_UNROLL)]
            m = [jnp.max(x, axis=-1, keepdims=True) for x in s]
            e = [jnp.exp(s[j] - m[j]) for j in range(ATTN_UNROLL)]
            l = [jnp.sum(x, axis=-1, keepdims=True) for x in e]
            acc = [jnp.dot(e[j].astype(BF16), v[j], preferred_element_type=F32) for j in range(ATTN_UNROLL)]
            for j in range(ATTN_UNROLL):
                ob_ref[rows(p * seq + tok[j], Q_BLOCK), :] = acc[j] / l[j]
                lb_ref[rows(p * seq + tok[j], Q_BLOCK), :] = jnp.broadcast_to(
                    m[j] + jnp.log(l[j]), (Q_BLOCK, HEAD_DIM))
            return carry

        lax.fori_loop(0, nblocks // ATTN_UNROLL, group, 0)

    def mix(c, carry):
        r0 = pl.multiple_of(c * rc, rc)
        lses = [lb_ref[pl.ds(p * seq + r0, rc), :] for p in range(nb)]
        mx = functools.reduce(jnp.maximum, lses)
        ws = [jnp.exp(x - mx) for x in lses]
        num = functools.reduce(lambda a, b: a + b,
                               [w * ob_ref[pl.ds(p * seq + r0, rc), :] for p, w in enumerate(ws)])
        den = functools.reduce(lambda a, b: a + b, ws)
        o_ref[0, pl.ds(r0, rc), :] = (num / den).astype(o_ref.dtype)
        return carry

    lax.fori_loop(0, seq // rc, mix, 0)


def _attention(proj, pos_b, invf, sgn, qg, kg):
    b, seq, _ = proj.shape
    ha = N_ATTN_HEADS
    blk = lambda off: pl.BlockSpec((1, seq, HEAD_DIM), lambda i, h: (i, 0, off + h))
    vec = pl.BlockSpec((1, HEAD_DIM), lambda i, h: (0, 0))
    nd = len(DILATIONS)
    assert seq % (ATTN_UNROLL * Q_BLOCK) == 0 and seq % ATTN_ROWS == 0
    return pl.pallas_call(
        functools.partial(_attn_kernel, seq=seq),
        grid=(b, ha),
        in_specs=[pl.BlockSpec((1, seq, HEAD_DIM), lambda i, h: (i, 0, 0)),
                  vec, vec, vec, vec, blk(0), blk(ha), blk(2 * ha)],
        out_specs=pl.BlockSpec((1, seq, HEAD_DIM), lambda i, h: (i, 0, h)),
        out_shape=jax.ShapeDtypeStruct((b, seq, ATTN_WIDTH), BF16),
        scratch_shapes=[pltpu.VMEM((seq, HEAD_DIM), F32),
                        pltpu.VMEM((seq, HEAD_DIM), F32),
                        pltpu.VMEM((2, seq, HEAD_DIM), F32),
                        pltpu.VMEM((3 * nd * seq, HEAD_DIM), BF16),
                        pltpu.VMEM((nd * seq, HEAD_DIM), F32),
                        pltpu.VMEM((nd * seq, HEAD_DIM), F32),
                        pltpu.VMEM((2, Q_BLOCK, 2 * Q_BLOCK), F32)],
        compiler_params=_cparams(("parallel", "arbitrary")),
        name="dilated_attn",
    )(pos_b, invf, sgn, qg, kg, proj, proj, proj)


def _softplus(x):
    return jnp.maximum(x, 0.0) + jnp.log1p(jnp.exp(-jnp.abs(x)))


def _delta_prep_kernel(ab_ref, alog_ref, dtb_ref, sel_ref, selr_ref, cwq_ref, cwk_ref, cwv_ref,
                       q_ref, k_ref, v_ref, hq_ref, hk_ref, hv_ref,
                       u_ref, w_ref, qd_ref, kt_ref, in_ref, dec_ref,
                       xp_ref, comp_ref, *, ts):
    i = pl.program_id(1)
    h = pl.program_id(2)
    npair = ts // PAIR
    ci = lax.broadcasted_iota(jnp.int32, (PAIR, PAIR), 0)
    cj = lax.broadcasted_iota(jnp.int32, (PAIR, PAIR), 1)
    same = (ci // CHUNK) == (cj // CHUNK)
    causal = same & (ci >= cj)
    strict = same & (ci > cj)
    eye = (ci == cj).astype(F32)

    @pl.when(h == 0)
    def _():
        lane = lax.broadcasted_iota(jnp.int32, (PAIR, LANES), 1)
        tril = causal.astype(F32)
        for p in range(npair):
            rs = slice(p * PAIR, (p + 1) * PAIR)
            ab = ab_ref[0, rs, :]
            g = -jnp.exp(alog_ref[...]) * _softplus(ab + dtb_ref[...])
            gc = jnp.dot(tril, g, preferred_element_type=F32, precision=lax.Precision.HIGHEST)
            comp = jnp.where(lane < N_DELTA_HEADS, gc, jax.nn.sigmoid(ab))
            for t, part in enumerate(_split_bf16(comp, 3)):
                comp_ref[rs, t * LANES:(t + 1) * LANES] = part

    def conv_silu(x_ref, halo_ref, cw_ref):
        halo = halo_ref[0]
        xp_ref[0:SUBLANES, :] = jnp.where(i > 0, halo, jnp.zeros_like(halo))
        xp_ref[SUBLANES:SUBLANES + ts, :] = x_ref[0]
        y = jnp.zeros((ts, HEAD_DIM), F32)
        for j in range(CONV_WIDTH):
            off = SUBLANES - (CONV_WIDTH - 1) + j
            y = y + cw_ref[j:j + 1, :] * xp_ref[off:off + ts, :]
        return y * jax.nn.sigmoid(y)

    ones2 = jnp.ones((2 * HEAD_DIM, HEAD_DIM), BF16)

    def l2n(x):
        ss = jnp.dot(jnp.concatenate(_split_bf16(x * x, 2), axis=1), ones2, preferred_element_type=F32)
        return x * lax.rsqrt(ss + NORM_EPS)

    q = l2n(conv_silu(q_ref, hq_ref, cwq_ref)) * (HEAD_DIM ** -0.5)
    k = l2n(conv_silu(k_ref, hk_ref, cwk_ref))
    v = conv_silu(v_ref, hv_ref, cwv_ref)
    rep = jnp.dot(comp_ref[...], sel_ref[0], preferred_element_type=F32)

    merge_masks = []
    s = 1
    while s < CHUNK:
        merge_masks.append(((ci // s) % 2 == 1) & ((cj // s) == (ci // s) - 1))
        s *= 2
    row = lax.broadcasted_iota(jnp.int32, (PAIR, LANES), 0)

    pairs = range(npair)
    rsl = [slice(p * PAIR, (p + 1) * PAIR) for p in pairs]
    gc = [rep[rs, :LANES] for rs in rsl]
    beta = [rep[rs, LANES:] for rs in rsl]
    grow = [lax.dot_general(selr_ref[0], comp_ref[rs, :], NT_DIMS, preferred_element_type=F32)
            for rs in rsl]
    gamma = [jnp.exp(jnp.where(causal, gc[p] - grow[p], NEG_INF)) for p in pairs]
    kb = [k[rsl[p]] * beta[p] for p in pairs]
    kbf = [k[rs].astype(BF16) for rs in rsl]
    kk = [lax.dot_general(kb[p].astype(BF16), kbf[p], NT_DIMS, preferred_element_type=F32) for p in pairs]
    a = [jnp.where(strict, kk[p] * gamma[p], 0.0) for p in pairs]
    t = [eye - jnp.where(merge_masks[0], a[p], 0.0) for p in pairs]
    for off in merge_masks[1:]:
        tb = [t[p].astype(BF16) for p in pairs]
        x = [jnp.dot(jnp.where(off, a[p], 0.0).astype(BF16), tb[p], preferred_element_type=F32).astype(BF16)
             for p in pairs]
        t = [t[p] - jnp.dot(tb[p], x[p], preferred_element_type=F32) for p in pairs]
    eg = [jnp.exp(gc[p]) for p in pairs]
    uw = [jnp.dot(t[p].astype(BF16),
                  jnp.concatenate([v[rsl[p]] * beta[p], kb[p] * eg[p]], axis=1).astype(BF16),
                  preferred_element_type=F32) for p in pairs]
    for p in pairs:
        rs = rsl[p]
        intra = lax.dot_general(q[rs].astype(BF16), kbf[p], NT_DIMS, preferred_element_type=F32) * gamma[p]
        g0 = gc[p][CHUNK - 1:CHUNK, :]
        g1 = gc[p][PAIR - 1:PAIR, :]
        glast = jnp.where(row < CHUNK, g0, g1)
        u_ref[0, rs, :] = uw[p][:, :HEAD_DIM]
        w_ref[0, rs, :] = uw[p][:, HEAD_DIM:].astype(BF16)
        qd_ref[0, rs, :] = (q[rs] * eg[p]).astype(BF16)
        kt_ref[0, rs, :] = (k[rs] * jnp.exp(glast - gc[p])).astype(BF16)
        in_ref[0, rs, :] = intra.astype(BF16)
        dec_ref[0, 0, 2 * p:2 * p + 1, :] = jnp.exp(g0)
        dec_ref[0, 0, 2 * p + 1:2 * p + 2, :] = jnp.exp(g1)


def _delta_prep(proj, ab, alog_vec, dtb_vec, sel, selr, conv_w):
    b, seq, _ = proj.shape
    hd = N_DELTA_HEADS
    ts = _pick(seq, (2048, 1024, 512, 256, 128))
    q0 = 3 * N_ATTN_HEADS
    tile = lambda off: pl.BlockSpec((1, ts, HEAD_DIM), lambda bi, i, h: (bi, i, off + h))
    halo = lambda off: pl.BlockSpec(
        (1, SUBLANES, HEAD_DIM),
        lambda bi, i, h: (bi, jnp.maximum(i * (ts // SUBLANES) - 1, 0), off + h))
    cw = lambda off: pl.BlockSpec((CONV_WIDTH, HEAD_DIM), lambda bi, i, h: (0, off + h))
    vec = pl.BlockSpec((1, LANES), lambda bi, i, h: (0, 0))
    out_tile = pl.BlockSpec((1, ts, HEAD_DIM), lambda bi, i, h: (bi, i, h))
    big = lambda dt: jax.ShapeDtypeStruct((b, seq, DELTA_WIDTH), dt)
    return pl.pallas_call(
        functools.partial(_delta_prep_kernel, ts=ts),
        grid=(b, seq // ts, hd),
        in_specs=[pl.BlockSpec((1, ts, LANES), lambda bi, i, h: (bi, i, 0)),
                  vec, vec,
                  pl.BlockSpec((1, 3 * LANES, 2 * LANES), lambda bi, i, h: (h, 0, 0)),
                  pl.BlockSpec((1, PAIR, 3 * LANES), lambda bi, i, h: (h, 0, 0)),
                  cw(0), cw(hd), cw(2 * hd),
                  tile(q0), tile(q0 + hd), tile(q0 + 2 * hd),
                  halo(q0), halo(q0 + hd), halo(q0 + 2 * hd)],
        out_specs=[out_tile, out_tile, out_tile, out_tile, out_tile,
                   pl.BlockSpec((1, 1, ts // CHUNK, HEAD_DIM), lambda bi, i, h: (bi, h, i, 0))],
        out_shape=[big(F32), big(BF16), big(BF16), big(BF16), big(BF16),
                   jax.ShapeDtypeStruct((b, hd, seq // CHUNK, HEAD_DIM), F32)],
        scratch_shapes=[pltpu.VMEM((SUBLANES + ts, HEAD_DIM), F32),
                        pltpu.VMEM((ts, 3 * LANES), BF16)],
        compiler_params=_cparams(("parallel", "parallel", "arbitrary")),
        name="delta_prep",
    )(ab, alog_vec, dtb_vec, sel, selr, conv_w, conv_w, conv_w, proj, proj, proj, proj, proj, proj)


def _selectors():
    hd = N_DELTA_HEADS
    src = jnp.arange(3 * LANES) % LANES
    dst = jnp.arange(2 * LANES) // LANES
    head = jnp.arange(hd)[:, None, None]
    sel = (src[None, :, None] == head + hd * dst[None, None, :]).astype(BF16)
    selr = jnp.broadcast_to((src[None, None, :] == head).astype(BF16), (hd, PAIR, 3 * LANES))
    return sel, selr


def _delta_scan_kernel(u_ref, w_ref, qd_ref, kt_ref, in_ref, dec_ref, z_ref, gain_ref, o_ref, s_ref,
                       *, nb, npair):
    @pl.when(pl.program_id(0) == 0)
    def _():
        s_ref[...] = jnp.zeros_like(s_ref)

    gain = gain_ref[...]

    chains = [(bi, h) for bi in range(nb) for h in range(N_DELTA_HEADS)]
    cols = lambda h: slice(h * HEAD_DIM, (h + 1) * HEAD_DIM)

    def pair(pi, carry):
        for e in range(2):
            sl = pl.ds(pl.multiple_of(pi * PAIR + e * CHUNK, CHUNK), CHUNK)
            st = [s_ref[bi * N_DELTA_HEADS + h] for bi, h in chains]
            sb = [s.astype(BF16) for s in st]
            ws = [jnp.dot(w_ref[bi, sl, cols(h)], sb[n], preferred_element_type=F32)
                  for n, (bi, h) in enumerate(chains)]
            qs = [jnp.dot(qd_ref[bi, sl, cols(h)], sb[n], preferred_element_type=F32)
                  for n, (bi, h) in enumerate(chains)]
            vb = [(u_ref[bi, sl, cols(h)] - ws[n]).astype(BF16) for n, (bi, h) in enumerate(chains)]
            zero = jnp.zeros((CHUNK, HEAD_DIM), BF16)
            vpair = [jnp.concatenate([x, zero] if e == 0 else [zero, x], axis=0) for x in vb]
            o = [qs[n] + jnp.dot(in_ref[bi, sl, cols(h)], vpair[n], preferred_element_type=F32)
                 for n, (bi, h) in enumerate(chains)]
            for n, (bi, h) in enumerate(chains):
                dec = dec_ref[bi, h, pl.ds(2 * pi + e, 1), :]
                s_ref[bi * N_DELTA_HEADS + h] = st[n] * dec + lax.dot_general(
                    kt_ref[bi, sl, cols(h)], vb[n], TN_DIMS, preferred_element_type=F32)
            for n, (bi, h) in enumerate(chains):
                y = o[n] * lax.rsqrt(jnp.mean(o[n] * o[n], axis=-1, keepdims=True) + NORM_EPS) * gain
                z = z_ref[bi, sl, cols(h)]
                o_ref[bi, sl, cols(h)] = (y * (z * jax.nn.sigmoid(z))).astype(o_ref.dtype)
        return carry

    lax.fori_loop(0, npair, pair, 0)


def _delta_scan(u, w, qd, kt, intra, dec, proj, gain):
    b, seq, _ = u.shape
    cg = SUBLANES
    rows = cg * CHUNK
    zblk = (3 * ATTN_WIDTH + 3 * DELTA_WIDTH) // DELTA_WIDTH
    tile = pl.BlockSpec((b, rows, DELTA_WIDTH), lambda i: (0, i, 0))
    return pl.pallas_call(
        functools.partial(_delta_scan_kernel, nb=b, npair=rows // PAIR),
        grid=(seq // rows,),
        in_specs=[tile, tile, tile, tile, tile,
                  pl.BlockSpec((b, N_DELTA_HEADS, cg, HEAD_DIM), lambda i: (0, 0, i, 0)),
                  pl.BlockSpec((b, rows, DELTA_WIDTH), lambda i: (0, i, zblk)),
                  pl.BlockSpec((1, HEAD_DIM), lambda i: (0, 0))],
        out_specs=tile,
        out_shape=jax.ShapeDtypeStruct((b, seq, DELTA_WIDTH), BF16),
        scratch_shapes=[pltpu.VMEM((b * N_DELTA_HEADS, HEAD_DIM, HEAD_DIM), F32)],
        compiler_params=_cparams(("arbitrary",)),
        name="delta_scan",
    )(u, w, qd, kt, intra, dec, proj, gain)


def _outproj_kernel(oa_ref, od_ref, wa_ref, wd_ref, x_ref, mod_ref, o_ref):
    y = (jnp.dot(oa_ref[...], wa_ref[...].astype(BF16), preferred_element_type=F32)
         + jnp.dot(od_ref[...], wd_ref[...].astype(BF16), preferred_element_type=F32))
    o_ref[...] = x_ref[...] + mod_ref[0, 5:6, :] * y


def _outproj(oa, od, w_out, x2d, mod, layer, *, seq):
    t, d = x2d.shape
    tm = _pick(seq, (1024, 512, 256, 128))
    tn = _pick(d, (1024, 512, 256, 128))
    assert ATTN_WIDTH == DELTA_WIDTH
    return pl.pallas_call(
        _outproj_kernel,
        grid=(d // tn, t // tm),
        in_specs=[pl.BlockSpec((tm, ATTN_WIDTH), lambda j, i: (i, 0)),
                  pl.BlockSpec((tm, DELTA_WIDTH), lambda j, i: (i, 0)),
                  pl.BlockSpec((None, ATTN_WIDTH, tn), lambda j, i: (layer, 0, j)),
                  pl.BlockSpec((None, DELTA_WIDTH, tn), lambda j, i: (layer, 1, j)),
                  pl.BlockSpec((tm, tn), lambda j, i: (i, j)),
                  pl.BlockSpec((1, N_MOD, tn), lambda j, i: ((i * tm) // seq, 0, j))],
        out_specs=pl.BlockSpec((tm, tn), lambda j, i: (i, j)),
        out_shape=jax.ShapeDtypeStruct((t, d), F32),
        compiler_params=_cparams(("parallel", "parallel")),
        name="outproj",
    )(oa, od, w_out, w_out, x2d, mod)


def _layer(layer, x, mod, positions, ffn1_norm, ffn1_w_gate, ffn1_w_up, ffn1_w_down, mix_norm, w_in, conv_w,
           q_norm, k_norm, a_log, dt_bias, delta_out_norm, w_out, ffn2_norm, ffn2_w_gate, ffn2_w_up,
           ffn2_w_down):
    b, seq, d = x.shape
    t = b * seq
    x2d = x.reshape(t, d)
    row = lambda v: v.reshape(1, -1).astype(F32)

    x2d = _ffn(x2d, mod, row(ffn1_norm), ffn1_w_gate, ffn1_w_up, ffn1_w_down, layer, sub=0, seq=seq)

    n_ab = w_in.shape[2] - MAIN_PROJ
    w_ab = jnp.pad(w_in[layer, :, MAIN_PROJ:], ((0, 0), (0, LANES - n_ab))).astype(BF16)
    proj, ab = _inproj(x2d, mod, row(mix_norm), jnp.swapaxes(w_in, 1, 2), w_ab, layer, seq=seq)
    proj = proj.reshape(b, seq, MAIN_PROJ)
    ab = ab.reshape(b, seq, LANES)

    lane = jnp.arange(HEAD_DIM)
    inv_freq = ROPE_THETA ** (-(lane % ROPE_HALF).astype(F32) / ROPE_HALF)
    invf = jnp.where(lane < 2 * ROPE_HALF, inv_freq, 0.0).reshape(1, HEAD_DIM).astype(F32)
    sgn = jnp.where(lane < ROPE_HALF, -1.0, jnp.where(lane < 2 * ROPE_HALF, 1.0, 0.0)).reshape(1, HEAD_DIM)
    pos_b = jnp.broadcast_to(positions.astype(F32)[..., None], (b, seq, HEAD_DIM))
    oa = _attention(proj, pos_b, invf, sgn.astype(F32), row(q_norm), row(k_norm))

    lane_vec = lambda v: jnp.pad(v.astype(F32), (0, LANES - N_DELTA_HEADS)).reshape(1, LANES)
    sel, selr = _selectors()
    u, w, qd, kt, intra, dec = _delta_prep(proj, ab, lane_vec(a_log), lane_vec(dt_bias), sel, selr,
                                           conv_w.astype(F32))
    od = _delta_scan(u, w, qd, kt, intra, dec, proj, row(delta_out_norm))

    x2d = _outproj(oa.reshape(t, ATTN_WIDTH), od.reshape(t, DELTA_WIDTH), w_out, x2d, mod, layer, seq=seq)

    x2d = _ffn(x2d, mod, row(ffn2_norm), ffn2_w_gate, ffn2_w_up, ffn2_w_down, layer, sub=2, seq=seq)
    return x2d.reshape(b, seq, d)


def kernel(x, c, positions, w_ada, b_ada, ffn1_norm, ffn1_w_gate, ffn1_w_up, ffn1_w_down, mix_norm, w_in,
           conv_w, q_norm, k_norm, a_log, dt_bias, delta_out_norm, w_out, ffn2_norm, ffn2_w_gate, ffn2_w_up,
           ffn2_w_down):
    b, _, d = x.shape
    depth = w_ada.shape[0]
    c8 = jnp.pad(c.astype(F32), ((0, SUBLANES - b), (0, 0)))
    for l in range(depth):
        mod = _adaln(c8, w_ada[l], b_ada[l].reshape(1, -1))[:b].reshape(b, N_MOD, d)
        x = _layer(l, x, mod, positions, ffn1_norm[l], ffn1_w_gate, ffn1_w_up, ffn1_w_down,
                   mix_norm[l], w_in, conv_w[l], q_norm[l], k_norm[l], a_log[l], dt_bias[l],
                   delta_out_norm[l], w_out, ffn2_norm[l], ffn2_w_gate, ffn2_w_up, ffn2_w_down)
    return x
```

```python
import functools

import jax
import jax.numpy as jnp
from jax import lax
from jax.experimental import pallas as pl
from jax.experimental.pallas import tpu as pltpu

F32 = jnp.float32
BF16 = jnp.bfloat16

HEAD_DIM = 128
N_ATTN_HEADS = 8
N_DELTA_HEADS = 8
ATTN_WIDTH = N_ATTN_HEADS * HEAD_DIM
DELTA_WIDTH = N_DELTA_HEADS * HEAD_DIM
MIX_WIDTH = ATTN_WIDTH + DELTA_WIDTH
DILATIONS = (1, 4, 16)
Q_BLOCK = 128
ATTN_UNROLL = 16
ATTN_ROWS = 512
MOD_ROWS = 128
ROPE_THETA = 500000.0
ROPE_HALF = HEAD_DIM // 8
CONV_WIDTH = 4
CHUNK = 64
PAIR = 2 * CHUNK
NORM_EPS = 1e-6
N_MOD = 9
MAIN_PROJ = 3 * ATTN_WIDTH + 4 * DELTA_WIDTH
LANES = 128
SUBLANES = 8
VMEM_LIMIT = 60 * 1024 * 1024
NEG_INF = float("-inf")
NT_DIMS = (((1,), (1,)), ((), ()))
TN_DIMS = (((0,), (0,)), ((), ()))

assert PAIR == LANES


def _cparams(semantics):
    return pltpu.CompilerParams(dimension_semantics=semantics, vmem_limit_bytes=VMEM_LIMIT)


def _pick(n, candidates):
    for c in candidates:
        if n % c == 0:
            return c
    return n


def _split_bf16(x, terms):
    parts = []
    for _ in range(terms):
        p = x.astype(BF16)
        parts.append(p)
        x = x - p.astype(F32)
    return parts


def _adaln_kernel(c_ref, w_ref, b_ref, o_ref):
    c = c_ref[...]
    ca = (c * jax.nn.sigmoid(c)).astype(BF16)
    o_ref[...] = jnp.dot(ca, w_ref[...].astype(BF16), preferred_element_type=F32) + b_ref[...]


def _adaln(c8, w, b):
    d, n = w.shape
    tn = _pick(n, (1024, 512, 256, 128))
    return pl.pallas_call(
        _adaln_kernel,
        grid=(n // tn,),
        in_specs=[pl.BlockSpec((SUBLANES, d), lambda j: (0, 0)),
                  pl.BlockSpec((d, tn), lambda j: (0, j)),
                  pl.BlockSpec((1, tn), lambda j: (0, j))],
        out_specs=pl.BlockSpec((SUBLANES, tn), lambda j: (0, j)),
        out_shape=jax.ShapeDtypeStruct((SUBLANES, n), F32),
        compiler_params=_cparams(("parallel",)),
        name="adaln",
    )(c8, w, b)


def _modulate_store(h_ref, x_ref, gain_ref, mod_ref, sub):
    g = gain_ref[...] * (1.0 + mod_ref[0, 3 * sub + 1:3 * sub + 2, :])
    shift = mod_ref[0, 3 * sub:3 * sub + 1, :]
    for c in range(x_ref.shape[0] // MOD_ROWS):
        rs = slice(c * MOD_ROWS, (c + 1) * MOD_ROWS)
        x = x_ref[rs, :]
        r = lax.rsqrt(jnp.mean(x * x, axis=-1, keepdims=True) + NORM_EPS)
        h_ref[rs, :] = (x * r * g + shift).astype(BF16)


def _ffn_kernel(x_ref, mod_ref, gain_ref, wg_ref, wu_ref, wd_ref, o_ref, h_ref, *, sub, nf, nsplit):
    f = pl.program_id(1)

    @pl.when(f == 0)
    def _():
        _modulate_store(h_ref, x_ref, gain_ref, mod_ref, sub)
        o_ref[...] = jnp.zeros_like(o_ref)

    h = h_ref[...]
    g = jnp.dot(h, wg_ref[...].astype(BF16), preferred_element_type=F32)
    u = jnp.dot(h, wu_ref[...].astype(BF16), preferred_element_type=F32)
    a = (g * jax.nn.sigmoid(g) * u).astype(BF16)
    wd = wd_ref[...].astype(BF16)
    dn = o_ref.shape[1] // nsplit
    for n in range(nsplit):
        cs = slice(n * dn, (n + 1) * dn)
        o_ref[:, cs] += jnp.dot(a, wd[:, cs], preferred_element_type=F32)

    @pl.when(f == nf - 1)
    def _():
        gate = 0.5 * mod_ref[0, 3 * sub + 2:3 * sub + 3, :]
        o_ref[...] = x_ref[...] + gate * o_ref[...]


def _ffn(x2d, mod, gain, wg, wu, wd, layer, *, sub, seq):
    t, d = x2d.shape
    dff = wg.shape[2]
    tm = _pick(seq, (1024, 512, 256, 128))
    tf = _pick(dff, (256, 128))
    nf = dff // tf
    return pl.pallas_call(
        functools.partial(_ffn_kernel, sub=sub, nf=nf, nsplit=d // _pick(d, (512, 256, 128))),
        grid=(t // tm, nf),
        in_specs=[pl.BlockSpec((tm, d), lambda i, f: (i, 0)),
                  pl.BlockSpec((1, N_MOD, d), lambda i, f: ((i * tm) // seq, 0, 0)),
                  pl.BlockSpec((1, d), lambda i, f: (0, 0)),
                  pl.BlockSpec((None, d, tf), lambda i, f: (layer, 0, f)),
                  pl.BlockSpec((None, d, tf), lambda i, f: (layer, 0, f)),
                  pl.BlockSpec((None, tf, d), lambda i, f: (layer, f, 0))],
        out_specs=pl.BlockSpec((tm, d), lambda i, f: (i, 0)),
        out_shape=jax.ShapeDtypeStruct((t, d), F32),
        scratch_shapes=[pltpu.VMEM((tm, d), BF16)],
        compiler_params=_cparams(("parallel", "arbitrary")),
        name=f"ffn{sub}",
    )(x2d, mod, gain, wg, wu, wd)


def _inproj_kernel(x_ref, mod_ref, gain_ref, w_ref, wab_ref, o_ref, ab_ref, h_ref):
    n = pl.program_id(1)

    @pl.when(n == 0)
    def _():
        _modulate_store(h_ref, x_ref, gain_ref, mod_ref, 1)
        ab_ref[...] = jnp.dot(h_ref[...], wab_ref[...], preferred_element_type=F32)

    o_ref[...] = lax.dot_general(h_ref[...], w_ref[...].astype(BF16), NT_DIMS, preferred_element_type=F32)


def _inproj(x2d, mod, gain, w_in_t, w_ab, layer, *, seq):
    t, d = x2d.shape
    n = MAIN_PROJ
    tm = _pick(seq, (1024, 512, 256, 128))
    tn = _pick(n, (1024, 512, 256, 128))
    return pl.pallas_call(
        _inproj_kernel,
        grid=(t // tm, n // tn),
        in_specs=[pl.BlockSpec((tm, d), lambda i, j: (i, 0)),
                  pl.BlockSpec((1, N_MOD, d), lambda i, j: ((i * tm) // seq, 0, 0)),
                  pl.BlockSpec((1, d), lambda i, j: (0, 0)),
                  pl.BlockSpec((None, tn, d), lambda i, j: (layer, j, 0)),
                  pl.BlockSpec((d, LANES), lambda i, j: (0, 0))],
        out_specs=[pl.BlockSpec((tm, tn), lambda i, j: (i, j)),
                   pl.BlockSpec((tm, LANES), lambda i, j: (i, 0))],
        out_shape=[jax.ShapeDtypeStruct((t, n), F32), jax.ShapeDtypeStruct((t, LANES), F32)],
        scratch_shapes=[pltpu.VMEM((tm, d), BF16)],
        compiler_params=_cparams(("parallel", "arbitrary")),
        name="inproj",
    )(x2d, mod, gain, w_in_t, w_ab)


def _attn_kernel(pos_ref, invf_ref, sgn_ref, qg_ref, kg_ref, q_ref, k_ref, v_ref, o_ref,
                 cos_ref, sin_ref, src_ref, d_ref, ob_ref, lb_ref, bias_ref, *, seq):
    h = pl.program_id(1)
    nb = len(DILATIONS)
    rc = ATTN_ROWS

    @pl.when(h == 0)
    def _():
        ang = pos_ref[0] * invf_ref[...]
        cos_ref[...] = jnp.cos(ang)
        sin_ref[...] = jnp.sin(ang) * sgn_ref[...]

    lane = lax.broadcasted_iota(jnp.int32, (rc, HEAD_DIM), 1)

    mean_w = jnp.full((2 * HEAD_DIM, HEAD_DIM), 1.0 / HEAD_DIM, BF16)

    def norm_rope(x, gain, cs, sn):
        ms = jnp.dot(jnp.concatenate(_split_bf16(x * x, 2), axis=1), mean_w, preferred_element_type=F32)
        y = x * lax.rsqrt(ms + NORM_EPS) * gain
        partner = jnp.where(lane < ROPE_HALF, pltpu.roll(y, HEAD_DIM - ROPE_HALF, 1),
                            pltpu.roll(y, ROPE_HALF, 1))
        return y * cs + partner * sn

    def prep(c, carry):
        rs = pl.ds(pl.multiple_of(c * rc, rc), rc)
        cs, sn = cos_ref[rs, :], sin_ref[rs, :]
        src_ref[0, rs, :] = norm_rope(q_ref[0, rs, :], qg_ref[...], cs, sn) * (HEAD_DIM ** -0.5)
        src_ref[1, rs, :] = norm_rope(k_ref[0, rs, :], kg_ref[...], cs, sn)
        return carry

    lax.fori_loop(0, seq // rc, prep, 0)

    assert DILATIONS == (1, 4, 16)
    l4, l16 = seq // 4, seq // 16
    base = lambda p, t: (3 * p + t) * seq
    f4_ref = lb_ref
    for t in range(3):
        src = (lambda sl, t=t: src_ref[t, sl, :]) if t < 2 else (lambda sl: v_ref[0, sl, :])
        d_ref[base(0, t):base(0, t) + seq, :] = src(slice(0, seq)).astype(BF16)
        for r in range(4):
            x = src(pl.ds(r, l4, stride=4))
            f4_ref[t * seq + r * l4:t * seq + (r + 1) * l4, :] = x
            d_ref[base(1, t) + r * l4:base(1, t) + (r + 1) * l4, :] = x.astype(BF16)
        for r4 in range(4):
            for r2 in range(4):
                res = r4 + 4 * r2
                x = f4_ref[pl.ds(t * seq + r4 * l4 + r2, l16, stride=4), :]
                d_ref[base(2, t) + res * l16:base(2, t) + (res + 1) * l16, :] = x.astype(BF16)

    qi = lax.broadcasted_iota(jnp.int32, (Q_BLOCK, 2 * Q_BLOCK), 0)
    kj = lax.broadcasted_iota(jnp.int32, (Q_BLOCK, 2 * Q_BLOCK), 1)
    dist = qi + Q_BLOCK - kj
    bias_ref[0] = jnp.where((dist >= 0) & (dist <= Q_BLOCK), 0.0, NEG_INF)
    bias_ref[1] = jnp.where(kj <= qi, 0.0, NEG_INF)

    nblocks = seq // Q_BLOCK
    for p, d in enumerate(DILATIONS):
        nblk = nblocks // d
        assert nblk >= 2

        def rows(start, size, d=d):
            return pl.ds(start, size) if d == 1 else pl.ds(start, size, stride=d)

        def group(it, carry, d=d, nblk=nblk, p=p, rows=rows):
            qrow, krow, tok, first = [], [], [], []
            for j in range(ATTN_UNROLL):
                t = it * ATTN_UNROLL + j
                r = lax.div(t, nblk)
                n = lax.rem(t, nblk)
                qrow.append(pl.multiple_of((r * nblk + n) * Q_BLOCK, Q_BLOCK))
                krow.append(pl.multiple_of((r * nblk + jnp.maximum(n - 1, 0)) * Q_BLOCK, Q_BLOCK))
                tok.append(r + d * Q_BLOCK * n)
                first.append(jnp.where(n == 0, 1, 0))
            q = [d_ref[pl.ds(base(p, 0) + x, Q_BLOCK), :] for x in qrow]
            k = [d_ref[pl.ds(base(p, 1) + x, 2 * Q_BLOCK), :] for x in krow]
            v = [d_ref[pl.ds(base(p, 2) + x, 2 * Q_BLOCK), :] for x in krow]
            s = [lax.dot_general(q[j], k[j], NT_DIMS, preferred_element_type=F32) + bias_ref[first[j]]
                 for j in range(ATTN_UNROLL)]
            m = [jnp.max(x, axis=-1, keepdims=True) for x in s]
            e = [jnp.exp(s[j] - m[j]) for j in range(ATTN_UNROLL)]
            l = [jnp.sum(x, axis=-1, keepdims=True) for x in e]
            acc = [jnp.dot(e[j].astype(BF16), v[j], preferred_element_type=F32) for j in range(ATTN_UNROLL)]
            for j in range(ATTN_UNROLL):
                ob_ref[rows(p * seq + tok[j], Q_BLOCK), :] = acc[j] / l[j]
                lb_ref[rows(p * seq + tok[j], Q_BLOCK), :] = jnp.broadcast_to(
                    m[j] + jnp.log(l[j]), (Q_BLOCK, HEAD_DIM))
            return carry

        lax.fori_loop(0, nblocks // ATTN_UNROLL, group, 0)

    def mix(c, carry):
        r0 = pl.multiple_of(c * rc, rc)
        lses = [lb_ref[pl.ds(p * seq + r0, rc), :] for p in range(nb)]
        mx = functools.reduce(jnp.maximum, lses)
        ws = [jnp.exp(x - mx) for x in lses]
        num = functools.reduce(lambda a, b: a + b,
                               [w * ob_ref[pl.ds(p * seq + r0, rc), :] for p, w in enumerate(ws)])
        den = functools.reduce(lambda a, b: a + b, ws)
        o_ref[0, pl.ds(r0, rc), :] = (num / den).astype(o_ref.dtype)
        return carry

    lax.fori_loop(0, seq // rc, mix, 0)


def _attention(proj, pos_b, invf, sgn, qg, kg):
    b, seq, _ = proj.shape
    ha = N_ATTN_HEADS
    blk = lambda off: pl.BlockSpec((1, seq, HEAD_DIM), lambda i, h: (i, 0, off + h))
    vec = pl.BlockSpec((1, HEAD_DIM), lambda i, h: (0, 0))
    nd = len(DILATIONS)
    assert seq % (ATTN_UNROLL * Q_BLOCK) == 0 and seq % ATTN_ROWS == 0
    return pl.pallas_call(
        functools.partial(_attn_kernel, seq=seq),
        grid=(b, ha),
        in_specs=[pl.BlockSpec((1, seq, HEAD_DIM), lambda i, h: (i, 0, 0)),
                  vec, vec, vec, vec, blk(0), blk(ha), blk(2 * ha)],
        out_specs=pl.BlockSpec((1, seq, HEAD_DIM), lambda i, h: (i, 0, h)),
        out_shape=jax.ShapeDtypeStruct((b, seq, ATTN_WIDTH), BF16),
        scratch_shapes=[pltpu.VMEM((seq, HEAD_DIM), F32),
                        pltpu.VMEM((seq, HEAD_DIM), F32),
                        pltpu.VMEM((2, seq, HEAD_DIM), F32),
                        pltpu.VMEM((3 * nd * seq, HEAD_DIM), BF16),
                        pltpu.VMEM((nd * seq, HEAD_DIM), F32),
                        pltpu.VMEM((nd * seq, HEAD_DIM), F32),
                        pltpu.VMEM((2, Q_BLOCK, 2 * Q_BLOCK), F32)],
        compiler_params=_cparams(("parallel", "arbitrary")),
        name="dilated_attn",
    )(pos_b, invf, sgn, qg, kg, proj, proj, proj)


def _softplus(x):
    return jnp.maximum(x, 0.0) + jnp.log1p(jnp.exp(-jnp.abs(x)))


def _delta_prep_kernel(ab_ref, alog_ref, dtb_ref, sel_ref, selr_ref, cwq_ref, cwk_ref, cwv_ref,
                       q_ref, k_ref, v_ref, hq_ref, hk_ref, hv_ref,
                       u_ref, w_ref, qd_ref, kt_ref, in_ref, dec_ref,
                       xp_ref, comp_ref, *, ts):
    i = pl.program_id(1)
    h = pl.program_id(2)
    npair = ts // PAIR
    ci = lax.broadcasted_iota(jnp.int32, (PAIR, PAIR), 0)
    cj = lax.broadcasted_iota(jnp.int32, (PAIR, PAIR), 1)
    same = (ci // CHUNK) == (cj // CHUNK)
    causal = same & (ci >= cj)
    strict = same & (ci > cj)
    eye = (ci == cj).astype(F32)

    @pl.when(h == 0)
    def _():
        lane = lax.broadcasted_iota(jnp.int32, (PAIR, LANES), 1)
        tril = causal.astype(F32)
        for p in range(npair):
            rs = slice(p * PAIR, (p + 1) * PAIR)
            ab = ab_ref[0, rs, :]
            g = -jnp.exp(alog_ref[...]) * _softplus(ab + dtb_ref[...])
            gc = jnp.dot(tril, g, preferred_element_type=F32, precision=lax.Precision.HIGHEST)
            comp = jnp.where(lane < N_DELTA_HEADS, gc, jax.nn.sigmoid(ab))
            for t, part in enumerate(_split_bf16(comp, 3)):
                comp_ref[rs, t * LANES:(t + 1) * LANES] = part

    def conv_silu(x_ref, halo_ref, cw_ref):
        halo = halo_ref[0]
        xp_ref[0:SUBLANES, :] = jnp.where(i > 0, halo, jnp.zeros_like(halo))
        xp_ref[SUBLANES:SUBLANES + ts, :] = x_ref[0]
        y = jnp.zeros((ts, HEAD_DIM), F32)
        for j in range(CONV_WIDTH):
            off = SUBLANES - (CONV_WIDTH - 1) + j
            y = y + cw_ref[j:j + 1, :] * xp_ref[off:off + ts, :]
        return y * jax.nn.sigmoid(y)

    ones2 = jnp.ones((2 * HEAD_DIM, HEAD_DIM), BF16)

    def l2n(x):
        ss = jnp.dot(jnp.concatenate(_split_bf16(x * x, 2), axis=1), ones2, preferred_element_type=F32)
        return x * lax.rsqrt(ss + NORM_EPS)

    q = l2n(conv_silu(q_ref, hq_ref, cwq_ref)) * (HEAD_DIM ** -0.5)
    k = l2n(conv_silu(k_ref, hk_ref, cwk_ref))
    v = conv_silu(v_ref, hv_ref, cwv_ref)
    rep = jnp.dot(comp_ref[...], sel_ref[0], preferred_element_type=F32)

    merge_masks = []
    s = 1
    while s < CHUNK:
        merge_masks.append(((ci // s) % 2 == 1) & ((cj // s) == (ci // s) - 1))
        s *= 2
    row = lax.broadcasted_iota(jnp.int32, (PAIR, LANES), 0)

    pairs = range(npair)
    rsl = [slice(p * PAIR, (p + 1) * PAIR) for p in pairs]
    gc = [rep[rs, :LANES] for rs in rsl]
    beta = [rep[rs, LANES:] for rs in rsl]
    grow = [lax.dot_general(selr_ref[0], comp_ref[rs, :], NT_DIMS, preferred_element_type=F32)
            for rs in rsl]
    gamma = [jnp.exp(jnp.where(causal, gc[p] - grow[p], NEG_INF)) for p in pairs]
    kb = [k[rsl[p]] * beta[p] for p in pairs]
    kbf = [k[rs].astype(BF16) for rs in rsl]
    kk = [lax.dot_general(kb[p].astype(BF16), kbf[p], NT_DIMS, preferred_element_type=F32) for p in pairs]
    a = [jnp.where(strict, kk[p] * gamma[p], 0.0) for p in pairs]
    t = [eye - jnp.where(merge_masks[0], a[p], 0.0) for p in pairs]
    for off in merge_masks[1:]:
        tb = [t[p].astype(BF16) for p in pairs]
        x = [jnp.dot(jnp.where(off, a[p], 0.0).astype(BF16), tb[p], preferred_element_type=F32).astype(BF16)
             for p in pairs]
        t = [t[p] - jnp.dot(tb[p], x[p], preferred_element_type=F32) for p in pairs]
    eg = [jnp.exp(gc[p]) for p in pairs]
    uw = [jnp.dot(t[p].astype(BF16),
                  jnp.concatenate([v[rsl[p]] * beta[p], kb[p] * eg[p]], axis=1).astype(BF16),
                  preferred_element_type=F32) for p in pairs]
    for p in pairs:
        rs = rsl[p]
        intra = lax.dot_general(q[rs].astype(BF16), kbf[p], NT_DIMS, preferred_element_type=F32) * gamma[p]
        g0 = gc[p][CHUNK - 1:CHUNK, :]
        g1 = gc[p][PAIR - 1:PAIR, :]
        glast = jnp.where(row < CHUNK, g0, g1)
        u_ref[0, rs, :] = uw[p][:, :HEAD_DIM]
        w_ref[0, rs, :] = uw[p][:, HEAD_DIM:].astype(BF16)
        qd_ref[0, rs, :] = (q[rs] * eg[p]).astype(BF16)
        kt_ref[0, rs, :] = (k[rs] * jnp.exp(glast - gc[p])).astype(BF16)
        in_ref[0, rs, :] = intra.astype(BF16)
        dec_ref[0, 0, 2 * p:2 * p + 1, :] = jnp.exp(g0)
        dec_ref[0, 0, 2 * p + 1:2 * p + 2, :] = jnp.exp(g1)


def _delta_prep(proj, ab, alog_vec, dtb_vec, sel, selr, conv_w):
    b, seq, _ = proj.shape
    hd = N_DELTA_HEADS
    ts = _pick(seq, (2048, 1024, 512, 256, 128))
    q0 = 3 * N_ATTN_HEADS
    tile = lambda off: pl.BlockSpec((1, ts, HEAD_DIM), lambda bi, i, h: (bi, i, off + h))
    halo = lambda off: pl.BlockSpec(
        (1, SUBLANES, HEAD_DIM),
        lambda bi, i, h: (bi, jnp.maximum(i * (ts // SUBLANES) - 1, 0), off + h))
    cw = lambda off: pl.BlockSpec((CONV_WIDTH, HEAD_DIM), lambda bi, i, h: (0, off + h))
    vec = pl.BlockSpec((1, LANES), lambda bi, i, h: (0, 0))
    out_tile = pl.BlockSpec((1, ts, HEAD_DIM), lambda bi, i, h: (bi, i, h))
    big = lambda dt: jax.ShapeDtypeStruct((b, seq, DELTA_WIDTH), dt)
    return pl.pallas_call(
        functools.partial(_delta_prep_kernel, ts=ts),
        grid=(b, seq // ts, hd),
        in_specs=[pl.BlockSpec((1, ts, LANES), lambda bi, i, h: (bi, i, 0)),
                  vec, vec,
                  pl.BlockSpec((1, 3 * LANES, 2 * LANES), lambda bi, i, h: (h, 0, 0)),
                  pl.BlockSpec((1, PAIR, 3 * LANES), lambda bi, i, h: (h, 0, 0)),
                  cw(0), cw(hd), cw(2 * hd),
                  tile(q0), tile(q0 + hd), tile(q0 + 2 * hd),
                  halo(q0), halo(q0 + hd), halo(q0 + 2 * hd)],
        out_specs=[out_tile, out_tile, out_tile, out_tile, out_tile,
                   pl.BlockSpec((1, 1, ts // CHUNK, HEAD_DIM), lambda bi, i, h: (bi, h, i, 0))],
        out_shape=[big(F32), big(BF16), big(BF16), big(BF16), big(BF16),
                   jax.ShapeDtypeStruct((b, hd, seq // CHUNK, HEAD_DIM), F32)],
        scratch_shapes=[pltpu.VMEM((SUBLANES + ts, HEAD_DIM), F32),
                        pltpu.VMEM((ts, 3 * LANES), BF16)],
        compiler_params=_cparams(("parallel", "parallel", "arbitrary")),
        name="delta_prep",
    )(ab, alog_vec, dtb_vec, sel, selr, conv_w, conv_w, conv_w, proj, proj, proj, proj, proj, proj)


def _selectors():
    hd = N_DELTA_HEADS
    src = jnp.arange(3 * LANES) % LANES
    dst = jnp.arange(2 * LANES) // LANES
    head = jnp.arange(hd)[:, None, None]
    sel = (src[None, :, None] == head + hd * dst[None, None, :]).astype(BF16)
    selr = jnp.broadcast_to((src[None, None, :] == head).astype(BF16), (hd, PAIR, 3 * LANES))
    return sel, selr


def _delta_scan_kernel(u_ref, w_ref, qd_ref, kt_ref, in_ref, dec_ref, z_ref, gain_ref, o_ref, s_ref,
                       *, nb, npair):
    @pl.when(pl.program_id(0) == 0)
    def _():
        s_ref[...] = jnp.zeros_like(s_ref)

    gain = gain_ref[...]

    chains = [(bi, h) for bi in range(nb) for h in range(N_DELTA_HEADS)]
    cols = lambda h: slice(h * HEAD_DIM, (h + 1) * HEAD_DIM)

    def pair(pi, carry):
        for e in range(2):
            sl = pl.ds(pl.multiple_of(pi * PAIR + e * CHUNK, CHUNK), CHUNK)
            st = [s_ref[bi * N_DELTA_HEADS + h] for bi, h in chains]
            sb = [s.astype(BF16) for s in st]
            ws = [jnp.dot(w_ref[bi, sl, cols(h)], sb[n], preferred_element_type=F32)
                  for n, (bi, h) in enumerate(chains)]
            qs = [jnp.dot(qd_ref[bi, sl, cols(h)], sb[n], preferred_element_type=F32)
                  for n, (bi, h) in enumerate(chains)]
            vb = [(u_ref[bi, sl, cols(h)] - ws[n]).astype(BF16) for n, (bi, h) in enumerate(chains)]
            zero = jnp.zeros((CHUNK, HEAD_DIM), BF16)
            vpair = [jnp.concatenate([x, zero] if e == 0 else [zero, x], axis=0) for x in vb]
            o = [qs[n] + jnp.dot(in_ref[bi, sl, cols(h)], vpair[n], preferred_element_type=F32)
                 for n, (bi, h) in enumerate(chains)]
            for n, (bi, h) in enumerate(chains):
                dec = dec_ref[bi, h, pl.ds(2 * pi + e, 1), :]
                s_ref[bi * N_DELTA_HEADS + h] = st[n] * dec + lax.dot_general(
                    kt_ref[bi, sl, cols(h)], vb[n], TN_DIMS, preferred_element_type=F32)
            for n, (bi, h) in enumerate(chains):
                y = o[n] * lax.rsqrt(jnp.mean(o[n] * o[n], axis=-1, keepdims=True) + NORM_EPS) * gain
                z = z_ref[bi, sl, cols(h)]
                o_ref[bi, sl, cols(h)] = (y * (z * jax.nn.sigmoid(z))).astype(o_ref.dtype)
        return carry

    lax.fori_loop(0, npair, pair, 0)


def _delta_scan(u, w, qd, kt, intra, dec, proj, gain):
    b, seq, _ = u.shape
    cg = SUBLANES
    rows = cg * CHUNK
    zblk = (3 * ATTN_WIDTH + 3 * DELTA_WIDTH) // DELTA_WIDTH
    tile = pl.BlockSpec((b, rows, DELTA_WIDTH), lambda i: (0, i, 0))
    return pl.pallas_call(
        functools.partial(_delta_scan_kernel, nb=b, npair=rows // PAIR),
        grid=(seq // rows,),
        in_specs=[tile, tile, tile, tile, tile,
                  pl.BlockSpec((b, N_DELTA_HEADS, cg, HEAD_DIM), lambda i: (0, 0, i, 0)),
                  pl.BlockSpec((b, rows, DELTA_WIDTH), lambda i: (0, i, zblk)),
                  pl.BlockSpec((1, HEAD_DIM), lambda i: (0, 0))],
        out_specs=tile,
        out_shape=jax.ShapeDtypeStruct((b, seq, DELTA_WIDTH), BF16),
        scratch_shapes=[pltpu.VMEM((b * N_DELTA_HEADS, HEAD_DIM, HEAD_DIM), F32)],
        compiler_params=_cparams(("arbitrary",)),
        name="delta_scan",
    )(u, w, qd, kt, intra, dec, proj, gain)


def _outproj_kernel(oa_ref, od_ref, wa_ref, wd_ref, x_ref, mod_ref, o_ref):
    y = (jnp.dot(oa_ref[...], wa_ref[...].astype(BF16), preferred_element_type=F32)
         + jnp.dot(od_ref[...], wd_ref[...].astype(BF16), preferred_element_type=F32))
    o_ref[...] = x_ref[...] + mod_ref[0, 5:6, :] * y


def _outproj(oa, od, w_out, x2d, mod, layer, *, seq):
    t, d = x2d.shape
    tm = _pick(seq, (1024, 512, 256, 128))
    tn = _pick(d, (1024, 512, 256, 128))
    assert ATTN_WIDTH == DELTA_WIDTH
    return pl.pallas_call(
        _outproj_kernel,
        grid=(d // tn, t // tm),
        in_specs=[pl.BlockSpec((tm, ATTN_WIDTH), lambda j, i: (i, 0)),
                  pl.BlockSpec((tm, DELTA_WIDTH), lambda j, i: (i, 0)),
                  pl.BlockSpec((None, ATTN_WIDTH, tn), lambda j, i: (layer, 0, j)),
                  pl.BlockSpec((None, DELTA_WIDTH, tn), lambda j, i: (layer, 1, j)),
                  pl.BlockSpec((tm, tn), lambda j, i: (i, j)),
                  pl.BlockSpec((1, N_MOD, tn), lambda j, i: ((i * tm) // seq, 0, j))],
        out_specs=pl.BlockSpec((tm, tn), lambda j, i: (i, j)),
        out_shape=jax.ShapeDtypeStruct((t, d), F32),
        compiler_params=_cparams(("parallel", "parallel")),
        name="outproj",
    )(oa, od, w_out, w_out, x2d, mod)


def _layer(layer, x, mod, positions, ffn1_norm, ffn1_w_gate, ffn1_w_up, ffn1_w_down, mix_norm, w_in, conv_w,
           q_norm, k_norm, a_log, dt_bias, delta_out_norm, w_out, ffn2_norm, ffn2_w_gate, ffn2_w_up,
           ffn2_w_down):
    b, seq, d = x.shape
    t = b * seq
    x2d = x.reshape(t, d)
    row = lambda v: v.reshape(1, -1).astype(F32)

    x2d = _ffn(x2d, mod, row(ffn1_norm), ffn1_w_gate, ffn1_w_up, ffn1_w_down, layer, sub=0, seq=seq)

    n_ab = w_in.shape[2] - MAIN_PROJ
    w_ab = jnp.pad(w_in[layer, :, MAIN_PROJ:], ((0, 0), (0, LANES - n_ab))).astype(BF16)
    proj, ab = _inproj(x2d, mod, row(mix_norm), jnp.swapaxes(w_in, 1, 2), w_ab, layer, seq=seq)
    proj = proj.reshape(b, seq, MAIN_PROJ)
    ab = ab.reshape(b, seq, LANES)

    lane = jnp.arange(HEAD_DIM)
    inv_freq = ROPE_THETA ** (-(lane % ROPE_HALF).astype(F32) / ROPE_HALF)
    invf = jnp.where(lane < 2 * ROPE_HALF, inv_freq, 0.0).reshape(1, HEAD_DIM).astype(F32)
    sgn = jnp.where(lane < ROPE_HALF, -1.0, jnp.where(lane < 2 * ROPE_HALF, 1.0, 0.0)).reshape(1, HEAD_DIM)
    pos_b = jnp.broadcast_to(positions.astype(F32)[..., None], (b, seq, HEAD_DIM))
    oa = _attention(proj, pos_b, invf, sgn.astype(F32), row(q_norm), row(k_norm))

    lane_vec = lambda v: jnp.pad(v.astype(F32), (0, LANES - N_DELTA_HEADS)).reshape(1, LANES)
    sel, selr = _selectors()
    u, w, qd, kt, intra, dec = _delta_prep(proj, ab, lane_vec(a_log), lane_vec(dt_bias), sel, selr,
                                           conv_w.astype(F32))
    od = _delta_scan(u, w, qd, kt, intra, dec, proj, row(delta_out_norm))

    x2d = _outproj(oa.reshape(t, ATTN_WIDTH), od.reshape(t, DELTA_WIDTH), w_out, x2d, mod, layer, seq=seq)

    x2d = _ffn(x2d, mod, row(ffn2_norm), ffn2_w_gate, ffn2_w_up, ffn2_w_down, layer, sub=2, seq=seq)
    return x2d.reshape(b, seq, d)


def kernel(x, c, positions, w_ada, b_ada, ffn1_norm, ffn1_w_gate, ffn1_w_up, ffn1_w_down, mix_norm, w_in,
           conv_w, q_norm, k_norm, a_log, dt_bias, delta_out_norm, w_out, ffn2_norm, ffn2_w_gate, ffn2_w_up,
           ffn2_w_down):
    b, _, d = x.shape
    depth = w_ada.shape[0]
    c8 = jnp.pad(c.astype(F32), ((0, SUBLANES - b), (0, 0)))
    for l in range(depth):
        mod = _adaln(c8, w_ada[l], b_ada[l].reshape(1, -1))[:b].reshape(b, N_MOD, d)
        x = _layer(l, x, mod, positions, ffn1_norm[l], ffn1_w_gate, ffn1_w_up, ffn1_w_down,
                   mix_norm[l], w_in, conv_w[l], q_norm[l], k_norm[l], a_log[l], dt_bias[l],
                   delta_out_norm[l], w_out, ffn2_norm[l], ffn2_w_gate, ffn2_w_up, ffn2_w_down)
    return x
```

```python
import functools

import jax
import jax.numpy as jnp
from jax import lax
from jax.experimental import pallas as pl
from jax.experimental.pallas import tpu as pltpu

F32 = jnp.float32
BF16 = jnp.bfloat16

HEAD_DIM = 128
N_ATTN_HEADS = 8
N_DELTA_HEADS = 8
ATTN_WIDTH = N_ATTN_HEADS * HEAD_DIM
DELTA_WIDTH = N_DELTA_HEADS * HEAD_DIM
MIX_WIDTH = ATTN_WIDTH + DELTA_WIDTH
DILATIONS = (1, 4, 16)
Q_BLOCK = 128
ATTN_UNROLL = 16
ATTN_ROWS = 512
MOD_ROWS = 128
ROPE_THETA = 500000.0
ROPE_HALF = HEAD_DIM // 8
CONV_WIDTH = 4
CHUNK = 64
PAIR = 2 * CHUNK
NORM_EPS = 1e-6
N_MOD = 9
MAIN_PROJ = 3 * ATTN_WIDTH + 4 * DELTA_WIDTH
LANES = 128
SUBLANES = 8
VMEM_LIMIT = 60 * 1024 * 1024
NEG_INF = float("-inf")
NT_DIMS = (((1,), (1,)), ((), ()))
TN_DIMS = (((0,), (0,)), ((), ()))

assert PAIR == LANES


def _cparams(semantics):
    return pltpu.CompilerParams(dimension_semantics=semantics, vmem_limit_bytes=VMEM_LIMIT)


def _pick(n, candidates):
    for c in candidates:
        if n % c == 0:
            return c
    return n


def _split_bf16(x, terms):
    parts = []
    for _ in range(terms):
        p = x.astype(BF16)
        parts.append(p)
        x = x - p.astype(F32)
    return parts


def _adaln_kernel(c_ref, w_ref, b_ref, o_ref):
    c = c_ref[...]
    ca = (c * jax.nn.sigmoid(c)).astype(BF16)
    o_ref[...] = jnp.dot(ca, w_ref[...].astype(BF16), preferred_element_type=F32) + b_ref[...]


def _adaln(c8, w, b):
    d, n = w.shape
    tn = _pick(n, (1024, 512, 256, 128))
    return pl.pallas_call(
        _adaln_kernel,
        grid=(n // tn,),
        in_specs=[pl.BlockSpec((SUBLANES, d), lambda j: (0, 0)),
                  pl.BlockSpec((d, tn), lambda j: (0, j)),
                  pl.BlockSpec((1, tn), lambda j: (0, j))],
        out_specs=pl.BlockSpec((SUBLANES, tn), lambda j: (0, j)),
        out_shape=jax.ShapeDtypeStruct((SUBLANES, n), F32),
        compiler_params=_cparams(("parallel",)),
        name="adaln",
    )(c8, w, b)


def _modulate_store(h_ref, x_ref, gain_ref, mod_ref, sub):
    g = gain_ref[...] * (1.0 + mod_ref[0, 3 * sub + 1:3 * sub + 2, :])
    shift = mod_ref[0, 3 * sub:3 * sub + 1, :]
    for c in range(x_ref.shape[0] // MOD_ROWS):
        rs = slice(c * MOD_ROWS, (c + 1) * MOD_ROWS)
        x = x_ref[rs, :]
        r = lax.rsqrt(jnp.mean(x * x, axis=-1, keepdims=True) + NORM_EPS)
        h_ref[rs, :] = (x * r * g + shift).astype(BF16)


def _ffn_kernel(x_ref, mod_ref, gain_ref, wg_ref, wu_ref, wd_ref, o_ref, h_ref, *, sub, nf, nsplit):
    f = pl.program_id(1)

    @pl.when(f == 0)
    def _():
        _modulate_store(h_ref, x_ref, gain_ref, mod_ref, sub)
        o_ref[...] = jnp.zeros_like(o_ref)

    h = h_ref[...]
    g = jnp.dot(h, wg_ref[...].astype(BF16), preferred_element_type=F32)
    u = jnp.dot(h, wu_ref[...].astype(BF16), preferred_element_type=F32)
    a = (g * jax.nn.sigmoid(g) * u).astype(BF16)
    wd = wd_ref[...].astype(BF16)
    dn = o_ref.shape[1] // nsplit
    for n in range(nsplit):
        cs = slice(n * dn, (n + 1) * dn)
        o_ref[:, cs] += jnp.dot(a, wd[:, cs], preferred_element_type=F32)

    @pl.when(f == nf - 1)
    def _():
        gate = 0.5 * mod_ref[0, 3 * sub + 2:3 * sub + 3, :]
        o_ref[...] = x_ref[...] + gate * o_ref[...]


def _ffn(x2d, mod, gain, wg, wu, wd, layer, *, sub, seq):
    t, d = x2d.shape
    dff = wg.shape[2]
    tm = _pick(seq, (1024, 512, 256, 128))
    tf = _pick(dff, (256, 128))
    nf = dff // tf
    return pl.pallas_call(
        functools.partial(_ffn_kernel, sub=sub, nf=nf, nsplit=d // _pick(d, (512, 256, 128))),
        grid=(t // tm, nf),
        in_specs=[pl.BlockSpec((tm, d), lambda i, f: (i, 0)),
                  pl.BlockSpec((1, N_MOD, d), lambda i, f: ((i * tm) // seq, 0, 0)),
                  pl.BlockSpec((1, d), lambda i, f: (0, 0)),
                  pl.BlockSpec((None, d, tf), lambda i, f: (layer, 0, f)),
                  pl.BlockSpec((None, d, tf), lambda i, f: (layer, 0, f)),
                  pl.BlockSpec((None, tf, d), lambda i, f: (layer, f, 0))],
        out_specs=pl.BlockSpec((tm, d), lambda i, f: (i, 0)),
        out_shape=jax.ShapeDtypeStruct((t, d), F32),
        scratch_shapes=[pltpu.VMEM((tm, d), BF16)],
        compiler_params=_cparams(("parallel", "arbitrary")),
        name=f"ffn{sub}",
    )(x2d, mod, gain, wg, wu, wd)


def _inproj_kernel(x_ref, mod_ref, gain_ref, w_ref, wab_ref, o_ref, ab_ref, h_ref):
    n = pl.program_id(1)

    @pl.when(n == 0)
    def _():
        _modulate_store(h_ref, x_ref, gain_ref, mod_ref, 1)
        ab_ref[...] = jnp.dot(h_ref[...], wab_ref[...], preferred_element_type=F32)

    o_ref[...] = lax.dot_general(h_ref[...], w_ref[...].astype(BF16), NT_DIMS, preferred_element_type=F32)


def _inproj(x2d, mod, gain, w_in_t, w_ab, layer, *, seq):
    t, d = x2d.shape
    n = MAIN_PROJ
    tm = _pick(seq, (1024, 512, 256, 128))
    tn = _pick(n, (1024, 512, 256, 128))
    return pl.pallas_call(
        _inproj_kernel,
        grid=(t // tm, n // tn),
        in_specs=[pl.BlockSpec((tm, d), lambda i, j: (i, 0)),
                  pl.BlockSpec((1, N_MOD, d), lambda i, j: ((i * tm) // seq, 0, 0)),
                  pl.BlockSpec((1, d), lambda i, j: (0, 0)),
                  pl.BlockSpec((None, tn, d), lambda i, j: (layer, j, 0)),
                  pl.BlockSpec((d, LANES), lambda i, j: (0, 0))],
        out_specs=[pl.BlockSpec((tm, tn), lambda i, j: (i, j)),
                   pl.BlockSpec((tm, LANES), lambda i, j: (i, 0))],
        out_shape=[jax.ShapeDtypeStruct((t, n), F32), jax.ShapeDtypeStruct((t, LANES), F32)],
        scratch_shapes=[pltpu.VMEM((tm, d), BF16)],
        compiler_params=_cparams(("parallel", "arbitrary")),
        name="inproj",
    )(x2d, mod, gain, w_in_t, w_ab)


def _attn_kernel(pos_ref, invf_ref, sgn_ref, qg_ref, kg_ref, q_ref, k_ref, v_ref, o_ref,
                 cos_ref, sin_ref, src_ref, d_ref, ob_ref, lb_ref, bias_ref, *, seq):
    h = pl.program_id(1)
    nb = len(DILATIONS)
    rc = ATTN_ROWS

    @pl.when(h == 0)
    def _():
        ang = pos_ref[0] * invf_ref[...]
        cos_ref[...] = jnp.cos(ang)
        sin_ref[...] = jnp.sin(ang) * sgn_ref[...]

    lane = lax.broadcasted_iota(jnp.int32, (rc, HEAD_DIM), 1)

    mean_w = jnp.full((2 * HEAD_DIM, HEAD_DIM), 1.0 / HEAD_DIM, BF16)

    def norm_rope(x, gain, cs, sn):
        ms = jnp.dot(jnp.concatenate(_split_bf16(x * x, 2), axis=1), mean_w, preferred_element_type=F32)
        y = x * lax.rsqrt(ms + NORM_EPS) * gain
        partner = jnp.where(lane < ROPE_HALF, pltpu.roll(y, HEAD_DIM - ROPE_HALF, 1),
                            pltpu.roll(y, ROPE_HALF, 1))
        return y * cs + partner * sn

    def prep(c, carry):
        rs = pl.ds(pl.multiple_of(c * rc, rc), rc)
        cs, sn = cos_ref[rs, :], sin_ref[rs, :]
        src_ref[0, rs, :] = norm_rope(q_ref[0, rs, :], qg_ref[...], cs, sn) * (HEAD_DIM ** -0.5)
        src_ref[1, rs, :] = norm_rope(k_ref[0, rs, :], kg_ref[...], cs, sn)
        return carry

    lax.fori_loop(0, seq // rc, prep, 0)

    assert DILATIONS == (1, 4, 16)
    l4, l16 = seq // 4, seq // 16
    base = lambda p, t: (3 * p + t) * seq
    f4_ref = lb_ref
    for t in range(3):
        src = (lambda sl, t=t: src_ref[t, sl, :]) if t < 2 else (lambda sl: v_ref[0, sl, :])
        d_ref[base(0, t):base(0, t) + seq, :] = src(slice(0, seq)).astype(BF16)
        for r in range(4):
            x = src(pl.ds(r, l4, stride=4))
            f4_ref[t * seq + r * l4:t * seq + (r + 1) * l4, :] = x
            d_ref[base(1, t) + r * l4:base(1, t) + (r + 1) * l4, :] = x.astype(BF16)
        for r4 in range(4):
            for r2 in range(4):
                res = r4 + 4 * r2
                x = f4_ref[pl.ds(t * seq + r4 * l4 + r2, l16, stride=4), :]
                d_ref[base(2, t) + res * l16:base(2, t) + (res + 1) * l16, :] = x.astype(BF16)

    qi = lax.broadcasted_iota(jnp.int32, (Q_BLOCK, 2 * Q_BLOCK), 0)
    kj = lax.broadcasted_iota(jnp.int32, (Q_BLOCK, 2 * Q_BLOCK), 1)
    dist = qi + Q_BLOCK - kj
    bias_ref[0] = jnp.where((dist >= 0) & (dist <= Q_BLOCK), 0.0, NEG_INF)
    bias_ref[1] = jnp.where(kj <= qi, 0.0, NEG_INF)

    nblocks = seq // Q_BLOCK
    for p, d in enumerate(DILATIONS):
        nblk = nblocks // d
        assert nblk >= 2

        def rows(start, size, d=d):
            return pl.ds(start, size) if d == 1 else pl.ds(start, size, stride=d)

        def group(it, carry, d=d, nblk=nblk, p=p, rows=rows):
            qrow, krow, tok, first = [], [], [], []
            for j in range(ATTN_UNROLL):
                t = it * ATTN_UNROLL + j
                r = lax.div(t, nblk)
                n = lax.rem(t, nblk)
                qrow.append(pl.multiple_of((r * nblk + n) * Q_BLOCK, Q_BLOCK))
                krow.append(pl.multiple_of((r * nblk + jnp.maximum(n - 1, 0)) * Q_BLOCK, Q_BLOCK))
                tok.append(r + d * Q_BLOCK * n)
                first.append(jnp.where(n == 0, 1, 0))
            q = [d_ref[pl.ds(base(p, 0) + x, Q_BLOCK), :] for x in qrow]
            k = [d_ref[pl.ds(base(p, 1) + x, 2 * Q_BLOCK), :] for x in krow]
            v = [d_ref[pl.ds(base(p, 2) + x, 2 * Q_BLOCK), :] for x in krow]
            s = [lax.dot_general(q[j], k[j], NT_DIMS, preferred_element_type=F32) + bias_ref[first[j]]
                 for j in range(ATTN_UNROLL)]
            m = [jnp.max(x, axis=-1, keepdims=True) for x in s]
            e = [jnp.exp(s[j] - m[j]) for j in range(ATTN_UNROLL)]
            l = [jnp.sum(x, axis=-1, keepdims=True) for x in e]
            acc = [jnp.dot(e[j].astype(BF16), v[j], preferred_element_type=F32) for j in range(ATTN_UNROLL)]
            for j in range(ATTN_UNROLL):
                ob_ref[rows(p * seq + tok[j], Q_BLOCK), :] = acc[j] / l[j]
                lb_ref[rows(p * seq + tok[j], Q_BLOCK), :] = jnp.broadcast_to(
                    m[j] + jnp.log(l[j]), (Q_BLOCK, HEAD_DIM))
            return carry

        lax.fori_loop(0, nblocks // ATTN_UNROLL, group, 0)

    def mix(c, carry):
        r0 = pl.multiple_of(c * rc, rc)
        lses = [lb_ref[pl.ds(p * seq + r0, rc), :] for p in range(nb)]
        mx = functools.reduce(jnp.maximum, lses)
        ws = [jnp.exp(x - mx) for x in lses]
        num = functools.reduce(lambda a, b: a + b,
                               [w * ob_ref[pl.ds(p * seq + r0, rc), :] for p, w in enumerate(ws)])
        den = functools.reduce(lambda a, b: a + b, ws)
        o_ref[0, pl.ds(r0, rc), :] = (num / den).astype(o_ref.dtype)
        return carry

    lax.fori_loop(0, seq // rc, mix, 0)


def _attention(proj, pos_b, invf, sgn, qg, kg):
    b, seq, _ = proj.shape
    ha = N_ATTN_HEADS
    blk = lambda off: pl.BlockSpec((1, seq, HEAD_DIM), lambda i, h: (i, 0, off + h))
    vec = pl.BlockSpec((1, HEAD_DIM), lambda i, h: (0, 0))
    nd = len(DILATIONS)
    assert seq % (ATTN_UNROLL * Q_BLOCK) == 0 and seq % ATTN_ROWS == 0
    return pl.pallas_call(
        functools.partial(_attn_kernel, seq=seq),
        grid=(b, ha),
        in_specs=[pl.BlockSpec((1, seq, HEAD_DIM), lambda i, h: (i, 0, 0)),
                  vec, vec, vec, vec, blk(0), blk(ha), blk(2 * ha)],
        out_specs=pl.BlockSpec((1, seq, HEAD_DIM), lambda i, h: (i, 0, h)),
        out_shape=jax.ShapeDtypeStruct((b, seq, ATTN_WIDTH), BF16),
        scratch_shapes=[pltpu.VMEM((seq, HEAD_DIM), F32),
                        pltpu.VMEM((seq, HEAD_DIM), F32),
                        pltpu.VMEM((2, seq, HEAD_DIM), F32),
                        pltpu.VMEM((3 * nd * seq, HEAD_DIM), BF16),
                        pltpu.VMEM((nd * seq, HEAD_DIM), F32),
                        pltpu.VMEM((nd * seq, HEAD_DIM), F32),
                        pltpu.VMEM((2, Q_BLOCK, 2 * Q_BLOCK), F32)],
        compiler_params=_cparams(("parallel", "arbitrary")),
        name="dilated_attn",
    )(pos_b, invf, sgn, qg, kg, proj, proj, proj)


def _softplus(x):
    return jnp.maximum(x, 0.0) + jnp.log1p(jnp.exp(-jnp.abs(x)))


def _delta_prep_kernel(ab_ref, alog_ref, dtb_ref, sel_ref, selr_ref, cwq_ref, cwk_ref, cwv_ref,
                       q_ref, k_ref, v_ref, hq_ref, hk_ref, hv_ref,
                       u_ref, w_ref, qd_ref, kt_ref, in_ref, dec_ref,
                       xp_ref, comp_ref, *, ts):
    i = pl.program_id(1)
    h = pl.program_id(2)
    npair = ts // PAIR
    ci = lax.broadcasted_iota(jnp.int32, (PAIR, PAIR), 0)
    cj = lax.broadcasted_iota(jnp.int32, (PAIR, PAIR), 1)
    same = (ci // CHUNK) == (cj // CHUNK)
    causal = same & (ci >= cj)
    strict = same & (ci > cj)
    eye = (ci == cj).astype(F32)

    @pl.when(h == 0)
    def _():
        lane = lax.broadcasted_iota(jnp.int32, (PAIR, LANES), 1)
        tril = causal.astype(F32)
        for p in range(npair):
            rs = slice(p * PAIR, (p + 1) * PAIR)
            ab = ab_ref[0, rs, :]
            g = -jnp.exp(alog_ref[...]) * _softplus(ab + dtb_ref[...])
            gc = jnp.dot(tril, g, preferred_element_type=F32, precision=lax.Precision.HIGHEST)
            comp = jnp.where(lane < N_DELTA_HEADS, gc, jax.nn.sigmoid(ab))
            for t, part in enumerate(_split_bf16(comp, 3)):
                comp_ref[rs, t * LANES:(t + 1) * LANES] = part

    def conv_silu(x_ref, halo_ref, cw_ref):
        halo = halo_ref[0]
        xp_ref[0:SUBLANES, :] = jnp.where(i > 0, halo, jnp.zeros_like(halo))
        xp_ref[SUBLANES:SUBLANES + ts, :] = x_ref[0]
        y = jnp.zeros((ts, HEAD_DIM), F32)
        for j in range(CONV_WIDTH):
            off = SUBLANES - (CONV_WIDTH - 1) + j
            y = y + cw_ref[j:j + 1, :] * xp_ref[off:off + ts, :]
        return y * jax.nn.sigmoid(y)

    ones2 = jnp.ones((2 * HEAD_DIM, HEAD_DIM), BF16)

    def l2n(x):
        ss = jnp.dot(jnp.concatenate(_split_bf16(x * x, 2), axis=1), ones2, preferred_element_type=F32)
        return x * lax.rsqrt(ss + NORM_EPS)

    q = l2n(conv_silu(q_ref, hq_ref, cwq_ref)) * (HEAD_DIM ** -0.5)
    k = l2n(conv_silu(k_ref, hk_ref, cwk_ref))
    v = conv_silu(v_ref, hv_ref, cwv_ref)
    rep = jnp.dot(comp_ref[...], sel_ref[0], preferred_element_type=F32)

    merge_masks = []
    s = 1
    while s < CHUNK:
        merge_masks.append(((ci // s) % 2 == 1) & ((cj // s) == (ci // s) - 1))
        s *= 2
    row = lax.broadcasted_iota(jnp.int32, (PAIR, LANES), 0)

    pairs = range(npair)
    rsl = [slice(p * PAIR, (p + 1) * PAIR) for p in pairs]
    gc = [rep[rs, :LANES] for rs in rsl]
    beta = [rep[rs, LANES:] for rs in rsl]
    grow = [lax.dot_general(selr_ref[0], comp_ref[rs, :], NT_DIMS, preferred_element_type=F32)
            for rs in rsl]
    gamma = [jnp.exp(jnp.where(causal, gc[p] - grow[p], NEG_INF)) for p in pairs]
    kb = [k[rsl[p]] * beta[p] for p in pairs]
    kbf = [k[rs].astype(BF16) for rs in rsl]
    kk = [lax.dot_general(kb[p].astype(BF16), kbf[p], NT_DIMS, preferred_element_type=F32) for p in pairs]
    a = [jnp.where(strict, kk[p] * gamma[p], 0.0) for p in pairs]
    t = [eye - jnp.where(merge_masks[0], a[p], 0.0) for p in pairs]
    for off in merge_masks[1:]:
        tb = [t[p].astype(BF16) for p in pairs]
        x = [jnp.dot(jnp.where(off, a[p], 0.0).astype(BF16), tb[p], preferred_element_type=F32).astype(BF16)
             for p in pairs]
        t = [t[p] - jnp.dot(tb[p], x[p], preferred_element_type=F32) for p in pairs]
    eg = [jnp.exp(gc[p]) for p in pairs]
    uw = [jnp.dot(t[p].astype(BF16),
                  jnp.concatenate([v[rsl[p]] * beta[p], kb[p] * eg[p]], axis=1).astype(BF16),
                  preferred_element_type=F32) for p in pairs]
    for p in pairs:
        rs = rsl[p]
        intra = lax.dot_general(q[rs].astype(BF16), kbf[p], NT_DIMS, preferred_element_type=F32) * gamma[p]
        g0 = gc[p][CHUNK - 1:CHUNK, :]
        g1 = gc[p][PAIR - 1:PAIR, :]
        glast = jnp.where(row < CHUNK, g0, g1)
        u_ref[0, rs, :] = uw[p][:, :HEAD_DIM]
        w_ref[0, rs, :] = uw[p][:, HEAD_DIM:].astype(BF16)
        qd_ref[0, rs, :] = (q[rs] * eg[p]).astype(BF16)
        kt_ref[0, rs, :] = (k[rs] * jnp.exp(glast - gc[p])).astype(BF16)
        in_ref[0, rs, :] = intra.astype(BF16)
        dec_ref[0, 0, 2 * p:2 * p + 1, :] = jnp.exp(g0)
        dec_ref[0, 0, 2 * p + 1:2 * p + 2, :] = jnp.exp(g1)


def _delta_prep(proj, ab, alog_vec, dtb_vec, sel, selr, conv_w):
    b, seq, _ = proj.shape
    hd = N_DELTA_HEADS
    ts = _pick(seq, (2048, 1024, 512, 256, 128))
    q0 = 3 * N_ATTN_HEADS
    tile = lambda off: pl.BlockSpec((1, ts, HEAD_DIM), lambda bi, i, h: (bi, i, off + h))
    halo = lambda off: pl.BlockSpec(
        (1, SUBLANES, HEAD_DIM),
        lambda bi, i, h: (bi, jnp.maximum(i * (ts // SUBLANES) - 1, 0), off + h))
    cw = lambda off: pl.BlockSpec((CONV_WIDTH, HEAD_DIM), lambda bi, i, h: (0, off + h))
    vec = pl.BlockSpec((1, LANES), lambda bi, i, h: (0, 0))
    out_tile = pl.BlockSpec((1, ts, HEAD_DIM), lambda bi, i, h: (bi, i, h))
    big = lambda dt: jax.ShapeDtypeStruct((b, seq, DELTA_WIDTH), dt)
    return pl.pallas_call(
        functools.partial(_delta_prep_kernel, ts=ts),
        grid=(b, seq // ts, hd),
        in_specs=[pl.BlockSpec((1, ts, LANES), lambda bi, i, h: (bi, i, 0)),
                  vec, vec,
                  pl.BlockSpec((1, 3 * LANES, 2 * LANES), lambda bi, i, h: (h, 0, 0)),
                  pl.BlockSpec((1, PAIR, 3 * LANES), lambda bi, i, h: (h, 0, 0)),
                  cw(0), cw(hd), cw(2 * hd),
                  tile(q0), tile(q0 + hd), tile(q0 + 2 * hd),
                  halo(q0), halo(q0 + hd), halo(q0 + 2 * hd)],
        out_specs=[out_tile, out_tile, out_tile, out_tile, out_tile,
                   pl.BlockSpec((1, 1, ts // CHUNK, HEAD_DIM), lambda bi, i, h: (bi, h, i, 0))],
        out_shape=[big(F32), big(BF16), big(BF16), big(BF16), big(BF16),
                   jax.ShapeDtypeStruct((b, hd, seq // CHUNK, HEAD_DIM), F32)],
        scratch_shapes=[pltpu.VMEM((SUBLANES + ts, HEAD_DIM), F32),
                        pltpu.VMEM((ts, 3 * LANES), BF16)],
        compiler_params=_cparams(("parallel", "parallel", "arbitrary")),
        name="delta_prep",
    )(ab, alog_vec, dtb_vec, sel, selr, conv_w, conv_w, conv_w, proj, proj, proj, proj, proj, proj)


def _selectors():
    hd = N_DELTA_HEADS
    src = jnp.arange(3 * LANES) % LANES
    dst = jnp.arange(2 * LANES) // LANES
    head = jnp.arange(hd)[:, None, None]
    sel = (src[None, :, None] == head + hd * dst[None, None, :]).astype(BF16)
    selr = jnp.broadcast_to((src[None, None, :] == head).astype(BF16), (hd, PAIR, 3 * LANES))
    return sel, selr


def _delta_scan_kernel(u_ref, w_ref, qd_ref, kt_ref, in_ref, dec_ref, z_ref, gain_ref, o_ref, s_ref,
                       *, nb, npair):
    @pl.when(pl.program_id(0) == 0)
    def _():
        s_ref[...] = jnp.zeros_like(s_ref)

    gain = gain_ref[...]

    chains = [(bi, h) for bi in range(nb) for h in range(N_DELTA_HEADS)]
    cols = lambda h: slice(h * HEAD_DIM, (h + 1) * HEAD_DIM)

    def pair(pi, carry):
        for e in range(2):
            sl = pl.ds(pl.multiple_of(pi * PAIR + e * CHUNK, CHUNK), CHUNK)
            st = [s_ref[bi * N_DELTA_HEADS + h] for bi, h in chains]
            sb = [s.astype(BF16) for s in st]
            ws = [jnp.dot(w_ref[bi, sl, cols(h)], sb[n], preferred_element_type=F32)
                  for n, (bi, h) in enumerate(chains)]
            qs = [jnp.dot(qd_ref[bi, sl, cols(h)], sb[n], preferred_element_type=F32)
                  for n, (bi, h) in enumerate(chains)]
            vb = [(u_ref[bi, sl, cols(h)] - ws[n]).astype(BF16) for n, (bi, h) in enumerate(chains)]
            zero = jnp.zeros((CHUNK, HEAD_DIM), BF16)
            vpair = [jnp.concatenate([x, zero] if e == 0 else [zero, x], axis=0) for x in vb]
            o = [qs[n] + jnp.dot(in_ref[bi, sl, cols(h)], vpair[n], preferred_element_type=F32)
                 for n, (bi, h) in enumerate(chains)]
            for n, (bi, h) in enumerate(chains):
                dec = dec_ref[bi, h, pl.ds(2 * pi + e, 1), :]
                s_ref[bi * N_DELTA_HEADS + h] = st[n] * dec + lax.dot_general(
                    kt_ref[bi, sl, cols(h)], vb[n], TN_DIMS, preferred_element_type=F32)
            for n, (bi, h) in enumerate(chains):
                y = o[n] * lax.rsqrt(jnp.mean(o[n] * o[n], axis=-1, keepdims=True) + NORM_EPS) * gain
                z = z_ref[bi, sl, cols(h)]
                o_ref[bi, sl, cols(h)] = (y * (z * jax.nn.sigmoid(z))).astype(o_ref.dtype)
        return carry

    lax.fori_loop(0, npair, pair, 0)


def _delta_scan(u, w, qd, kt, intra, dec, proj, gain):
    b, seq, _ = u.shape
    cg = SUBLANES
    rows = cg * CHUNK
    zblk = (3 * ATTN_WIDTH + 3 * DELTA_WIDTH) // DELTA_WIDTH
    tile = pl.BlockSpec((b, rows, DELTA_WIDTH), lambda i: (0, i, 0))
    return pl.pallas_call(
        functools.partial(_delta_scan_kernel, nb=b, npair=rows // PAIR),
        grid=(seq // rows,),
        in_specs=[tile, tile, tile, tile, tile,
                  pl.BlockSpec((b, N_DELTA_HEADS, cg, HEAD_DIM), lambda i: (0, 0, i, 0)),
                  pl.BlockSpec((b, rows, DELTA_WIDTH), lambda i: (0, i, zblk)),
                  pl.BlockSpec((1, HEAD_DIM), lambda i: (0, 0))],
        out_specs=tile,
        out_shape=jax.ShapeDtypeStruct((b, seq, DELTA_WIDTH), BF16),
        scratch_shapes=[pltpu.VMEM((b * N_DELTA_HEADS, HEAD_DIM, HEAD_DIM), F32)],
        compiler_params=_cparams(("arbitrary",)),
        name="delta_scan",
    )(u, w, qd, kt, intra, dec, proj, gain)


def _outproj_kernel(oa_ref, od_ref, wa_ref, wd_ref, x_ref, mod_ref, o_ref):
    y = (jnp.dot(oa_ref[...], wa_ref[...].astype(BF16), preferred_element_type=F32)
         + jnp.dot(od_ref[...], wd_ref[...].astype(BF16), preferred_element_type=F32))
    o_ref[...] = x_ref[...] + mod_ref[0, 5:6, :] * y


def _outproj(oa, od, w_out, x2d, mod, layer, *, seq):
    t, d = x2d.shape
    tm = _pick(seq, (512, 256, 128))
    tn = _pick(d, (2048, 1024, 512, 256, 128))
    assert ATTN_WIDTH == DELTA_WIDTH
    return pl.pallas_call(
        _outproj_kernel,
        grid=(d // tn, t // tm),
        in_specs=[pl.BlockSpec((tm, ATTN_WIDTH), lambda j, i: (i, 0)),
                  pl.BlockSpec((tm, DELTA_WIDTH), lambda j, i: (i, 0)),
                  pl.BlockSpec((None, ATTN_WIDTH, tn), lambda j, i: (layer, 0, j)),
                  pl.BlockSpec((None, DELTA_WIDTH, tn), lambda j, i: (layer, 1, j)),
                  pl.BlockSpec((tm, tn), lambda j, i: (i, j)),
                  pl.BlockSpec((1, N_MOD, tn), lambda j, i: ((i * tm) // seq, 0, j))],
        out_specs=pl.BlockSpec((tm, tn), lambda j, i: (i, j)),
        out_shape=jax.ShapeDtypeStruct((t, d), F32),
        compiler_params=_cparams(("parallel", "parallel")),
        name="outproj",
    )(oa, od, w_out, w_out, x2d, mod)


def _layer(layer, x, mod, positions, ffn1_norm, ffn1_w_gate, ffn1_w_up, ffn1_w_down, mix_norm, w_in, conv_w,
           q_norm, k_norm, a_log, dt_bias, delta_out_norm, w_out, ffn2_norm, ffn2_w_gate, ffn2_w_up,
           ffn2_w_down):
    b, seq, d = x.shape
    t = b * seq
    x2d = x.reshape(t, d)
    row = lambda v: v.reshape(1, -1).astype(F32)

    x2d = _ffn(x2d, mod, row(ffn1_norm), ffn1_w_gate, ffn1_w_up, ffn1_w_down, layer, sub=0, seq=seq)

    n_ab = w_in.shape[2] - MAIN_PROJ
    w_ab = jnp.pad(w_in[layer, :, MAIN_PROJ:], ((0, 0), (0, LANES - n_ab))).astype(BF16)
    proj, ab = _inproj(x2d, mod, row(mix_norm), jnp.swapaxes(w_in, 1, 2), w_ab, layer, seq=seq)
    proj = proj.reshape(b, seq, MAIN_PROJ)
    ab = ab.reshape(b, seq, LANES)

    lane = jnp.arange(HEAD_DIM)
    inv_freq = ROPE_THETA ** (-(lane % ROPE_HALF).astype(F32) / ROPE_HALF)
    invf = jnp.where(lane < 2 * ROPE_HALF, inv_freq, 0.0).reshape(1, HEAD_DIM).astype(F32)
    sgn = jnp.where(lane < ROPE_HALF, -1.0, jnp.where(lane < 2 * ROPE_HALF, 1.0, 0.0)).reshape(1, HEAD_DIM)
    pos_b = jnp.broadcast_to(positions.astype(F32)[..., None], (b, seq, HEAD_DIM))
    oa = _attention(proj, pos_b, invf, sgn.astype(F32), row(q_norm), row(k_norm))

    lane_vec = lambda v: jnp.pad(v.astype(F32), (0, LANES - N_DELTA_HEADS)).reshape(1, LANES)
    sel, selr = _selectors()
    u, w, qd, kt, intra, dec = _delta_prep(proj, ab, lane_vec(a_log), lane_vec(dt_bias), sel, selr,
                                           conv_w.astype(F32))
    od = _delta_scan(u, w, qd, kt, intra, dec, proj, row(delta_out_norm))

    x2d = _outproj(oa.reshape(t, ATTN_WIDTH), od.reshape(t, DELTA_WIDTH), w_out, x2d, mod, layer, seq=seq)

    x2d = _ffn(x2d, mod, row(ffn2_norm), ffn2_w_gate, ffn2_w_up, ffn2_w_down, layer, sub=2, seq=seq)
    return x2d.reshape(b, seq, d)


def kernel(x, c, positions, w_ada, b_ada, ffn1_norm, ffn1_w_gate, ffn1_w_up, ffn1_w_down, mix_norm, w_in,
           conv_w, q_norm, k_norm, a_log, dt_bias, delta_out_norm, w_out, ffn2_norm, ffn2_w_gate, ffn2_w_up,
           ffn2_w_down):
    b, _, d = x.shape
    depth = w_ada.shape[0]
    c8 = jnp.pad(c.astype(F32), ((0, SUBLANES - b), (0, 0)))
    for l in range(depth):
        mod = _adaln(c8, w_ada[l], b_ada[l].reshape(1, -1))[:b].reshape(b, N_MOD, d)
        x = _layer(l, x, mod, positions, ffn1_norm[l], ffn1_w_gate, ffn1_w_up, ffn1_w_down,
                   mix_norm[l], w_in, conv_w[l], q_norm[l], k_norm[l], a_log[l], dt_bias[l],
                   delta_out_norm[l], w_out, ffn2_norm[l], ffn2_w_gate, ffn2_w_up, ffn2_w_down)
    return x
```

```python
import functools

import jax
import jax.numpy as jnp
from jax import lax
from jax.experimental import pallas as pl
from jax.experimental.pallas import tpu as pltpu

F32 = jnp.float32
BF16 = jnp.bfloat16

HEAD_DIM = 128
N_ATTN_HEADS = 8
N_DELTA_HEADS = 8
ATTN_WIDTH = N_ATTN_HEADS * HEAD_DIM
DELTA_WIDTH = N_DELTA_HEADS * HEAD_DIM
MIX_WIDTH = ATTN_WIDTH + DELTA_WIDTH
DILATIONS = (1, 4, 16)
Q_BLOCK = 128
ATTN_UNROLL = 16
ATTN_ROWS = 4096
MOD_ROWS = 128
ROPE_THETA = 500000.0
ROPE_HALF = HEAD_DIM // 8
CONV_WIDTH = 4
CHUNK = 64
PAIR = 2 * CHUNK
NORM_EPS = 1e-6
N_MOD = 9
MAIN_PROJ = 3 * ATTN_WIDTH + 4 * DELTA_WIDTH
LANES = 128
SUBLANES = 8
VMEM_LIMIT = 60 * 1024 * 1024
NEG_INF = float("-inf")
NT_DIMS = (((1,), (1,)), ((), ()))
TN_DIMS = (((0,), (0,)), ((), ()))

assert PAIR == LANES


def _cparams(semantics):
    return pltpu.CompilerParams(dimension_semantics=semantics, vmem_limit_bytes=VMEM_LIMIT)


def _pick(n, candidates):
    for c in candidates:
        if n % c == 0:
            return c
    return n


def _split_bf16(x, terms):
    parts = []
    for _ in range(terms):
        p = x.astype(BF16)
        parts.append(p)
        x = x - p.astype(F32)
    return parts


def _adaln_kernel(c_ref, w_ref, b_ref, o_ref):
    c = c_ref[...]
    ca = (c * jax.nn.sigmoid(c)).astype(BF16)
    o_ref[...] = jnp.dot(ca, w_ref[...].astype(BF16), preferred_element_type=F32) + b_ref[...]


def _adaln(c8, w, b):
    d, n = w.shape
    tn = _pick(n, (1024, 512, 256, 128))
    return pl.pallas_call(
        _adaln_kernel,
        grid=(n // tn,),
        in_specs=[pl.BlockSpec((SUBLANES, d), lambda j: (0, 0)),
                  pl.BlockSpec((d, tn), lambda j: (0, j)),
                  pl.BlockSpec((1, tn), lambda j: (0, j))],
        out_specs=pl.BlockSpec((SUBLANES, tn), lambda j: (0, j)),
        out_shape=jax.ShapeDtypeStruct((SUBLANES, n), F32),
        compiler_params=_cparams(("parallel",)),
        name="adaln",
    )(c8, w, b)


def _modulate_store(h_ref, x_ref, gain_ref, mod_ref, sub):
    g = gain_ref[...] * (1.0 + mod_ref[0, 3 * sub + 1:3 * sub + 2, :])
    shift = mod_ref[0, 3 * sub:3 * sub + 1, :]
    for c in range(x_ref.shape[0] // MOD_ROWS):
        rs = slice(c * MOD_ROWS, (c + 1) * MOD_ROWS)
        x = x_ref[rs, :]
        r = lax.rsqrt(jnp.mean(x * x, axis=-1, keepdims=True) + NORM_EPS)
        h_ref[rs, :] = (x * r * g + shift).astype(BF16)


def _ffn_kernel(x_ref, mod_ref, gain_ref, wg_ref, wu_ref, wd_ref, o_ref, h_ref, *, sub, nf, nsplit):
    f = pl.program_id(1)

    @pl.when(f == 0)
    def _():
        _modulate_store(h_ref, x_ref, gain_ref, mod_ref, sub)
        o_ref[...] = jnp.zeros_like(o_ref)

    h = h_ref[...]
    g = jnp.dot(h, wg_ref[...].astype(BF16), preferred_element_type=F32)
    u = jnp.dot(h, wu_ref[...].astype(BF16), preferred_element_type=F32)
    a = (g * jax.nn.sigmoid(g) * u).astype(BF16)
    wd = wd_ref[...].astype(BF16)
    dn = o_ref.shape[1] // nsplit
    for n in range(nsplit):
        cs = slice(n * dn, (n + 1) * dn)
        o_ref[:, cs] += jnp.dot(a, wd[:, cs], preferred_element_type=F32)

    @pl.when(f == nf - 1)
    def _():
        gate = 0.5 * mod_ref[0, 3 * sub + 2:3 * sub + 3, :]
        o_ref[...] = x_ref[...] + gate * o_ref[...]


def _ffn(x2d, mod, gain, wg, wu, wd, layer, *, sub, seq):
    t, d = x2d.shape
    dff = wg.shape[2]
    tm = _pick(seq, (1024, 512, 256, 128))
    tf = _pick(dff, (256, 128))
    nf = dff // tf
    return pl.pallas_call(
        functools.partial(_ffn_kernel, sub=sub, nf=nf, nsplit=d // _pick(d, (512, 256, 128))),
        grid=(t // tm, nf),
        in_specs=[pl.BlockSpec((tm, d), lambda i, f: (i, 0)),
                  pl.BlockSpec((1, N_MOD, d), lambda i, f: ((i * tm) // seq, 0, 0)),
                  pl.BlockSpec((1, d), lambda i, f: (0, 0)),
                  pl.BlockSpec((None, d, tf), lambda i, f: (layer, 0, f)),
                  pl.BlockSpec((None, d, tf), lambda i, f: (layer, 0, f)),
                  pl.BlockSpec((None, tf, d), lambda i, f: (layer, f, 0))],
        out_specs=pl.BlockSpec((tm, d), lambda i, f: (i, 0)),
        out_shape=jax.ShapeDtypeStruct((t, d), F32),
        scratch_shapes=[pltpu.VMEM((tm, d), BF16)],
        compiler_params=_cparams(("parallel", "arbitrary")),
        name=f"ffn{sub}",
    )(x2d, mod, gain, wg, wu, wd)


def _inproj_kernel(x_ref, mod_ref, gain_ref, w_ref, wab_ref, o_ref, ab_ref, h_ref):
    n = pl.program_id(1)

    @pl.when(n == 0)
    def _():
        _modulate_store(h_ref, x_ref, gain_ref, mod_ref, 1)
        ab_ref[...] = jnp.dot(h_ref[...], wab_ref[...], preferred_element_type=F32)

    o_ref[...] = lax.dot_general(h_ref[...], w_ref[...].astype(BF16), NT_DIMS, preferred_element_type=F32)


def _inproj(x2d, mod, gain, w_in_t, w_ab, layer, *, seq):
    t, d = x2d.shape
    n = MAIN_PROJ
    tm = _pick(seq, (1024, 512, 256, 128))
    tn = _pick(n, (1024, 512, 256, 128))
    return pl.pallas_call(
        _inproj_kernel,
        grid=(t // tm, n // tn),
        in_specs=[pl.BlockSpec((tm, d), lambda i, j: (i, 0)),
                  pl.BlockSpec((1, N_MOD, d), lambda i, j: ((i * tm) // seq, 0, 0)),
                  pl.BlockSpec((1, d), lambda i, j: (0, 0)),
                  pl.BlockSpec((None, tn, d), lambda i, j: (layer, j, 0)),
                  pl.BlockSpec((d, LANES), lambda i, j: (0, 0))],
        out_specs=[pl.BlockSpec((tm, tn), lambda i, j: (i, j)),
                   pl.BlockSpec((tm, LANES), lambda i, j: (i, 0))],
        out_shape=[jax.ShapeDtypeStruct((t, n), F32), jax.ShapeDtypeStruct((t, LANES), F32)],
        scratch_shapes=[pltpu.VMEM((tm, d), BF16)],
        compiler_params=_cparams(("parallel", "arbitrary")),
        name="inproj",
    )(x2d, mod, gain, w_in_t, w_ab)


def _attn_kernel(pos_ref, invf_ref, sgn_ref, qg_ref, kg_ref, q_ref, k_ref, v_ref, o_ref,
                 cos_ref, sin_ref, src_ref, d_ref, ob_ref, lb_ref, bias_ref, *, seq):
    h = pl.program_id(1)
    nb = len(DILATIONS)
    rc = ATTN_ROWS

    @pl.when(h == 0)
    def _():
        ang = pos_ref[0] * invf_ref[...]
        cos_ref[...] = jnp.cos(ang)
        sin_ref[...] = jnp.sin(ang) * sgn_ref[...]

    lane = lax.broadcasted_iota(jnp.int32, (rc, HEAD_DIM), 1)

    mean_w = jnp.full((2 * HEAD_DIM, HEAD_DIM), 1.0 / HEAD_DIM, BF16)

    def norm_rope(x, gain, cs, sn):
        ms = jnp.dot(jnp.concatenate(_split_bf16(x * x, 2), axis=1), mean_w, preferred_element_type=F32)
        y = x * lax.rsqrt(ms + NORM_EPS) * gain
        partner = jnp.where(lane < ROPE_HALF, pltpu.roll(y, HEAD_DIM - ROPE_HALF, 1),
                            pltpu.roll(y, ROPE_HALF, 1))
        return y * cs + partner * sn

    def prep(c, carry):
        rs = pl.ds(pl.multiple_of(c * rc, rc), rc)
        cs, sn = cos_ref[rs, :], sin_ref[rs, :]
        src_ref[0, rs, :] = norm_rope(q_ref[0, rs, :], qg_ref[...], cs, sn) * (HEAD_DIM ** -0.5)
        src_ref[1, rs, :] = norm_rope(k_ref[0, rs, :], kg_ref[...], cs, sn)
        return carry

    lax.fori_loop(0, seq // rc, prep, 0)

    assert DILATIONS == (1, 4, 16)
    l4, l16 = seq // 4, seq // 16
    base = lambda p, t: (3 * p + t) * seq
    f4_ref = lb_ref
    for t in range(3):
        src = (lambda sl, t=t: src_ref[t, sl, :]) if t < 2 else (lambda sl: v_ref[0, sl, :])
        d_ref[base(0, t):base(0, t) + seq, :] = src(slice(0, seq)).astype(BF16)
        for r in range(4):
            x = src(pl.ds(r, l4, stride=4))
            f4_ref[t * seq + r * l4:t * seq + (r + 1) * l4, :] = x
            d_ref[base(1, t) + r * l4:base(1, t) + (r + 1) * l4, :] = x.astype(BF16)
        for r4 in range(4):
            for r2 in range(4):
                res = r4 + 4 * r2
                x = f4_ref[pl.ds(t * seq + r4 * l4 + r2, l16, stride=4), :]
                d_ref[base(2, t) + res * l16:base(2, t) + (res + 1) * l16, :] = x.astype(BF16)

    qi = lax.broadcasted_iota(jnp.int32, (Q_BLOCK, 2 * Q_BLOCK), 0)
    kj = lax.broadcasted_iota(jnp.int32, (Q_BLOCK, 2 * Q_BLOCK), 1)
    dist = qi + Q_BLOCK - kj
    bias_ref[0] = jnp.where((dist >= 0) & (dist <= Q_BLOCK), 0.0, NEG_INF)
    bias_ref[1] = jnp.where(kj <= qi, 0.0, NEG_INF)

    nblocks = seq // Q_BLOCK
    for p, d in enumerate(DILATIONS):
        nblk = nblocks // d
        assert nblk >= 2

        def rows(start, size, d=d):
            return pl.ds(start, size) if d == 1 else pl.ds(start, size, stride=d)

        def group(it, carry, d=d, nblk=nblk, p=p, rows=rows):
            qrow, krow, tok, first = [], [], [], []
            for j in range(ATTN_UNROLL):
                t = it * ATTN_UNROLL + j
                r = lax.div(t, nblk)
                n = lax.rem(t, nblk)
                qrow.append(pl.multiple_of((r * nblk + n) * Q_BLOCK, Q_BLOCK))
                krow.append(pl.multiple_of((r * nblk + jnp.maximum(n - 1, 0)) * Q_BLOCK, Q_BLOCK))
                tok.append(r + d * Q_BLOCK * n)
                first.append(jnp.where(n == 0, 1, 0))
            q = [d_ref[pl.ds(base(p, 0) + x, Q_BLOCK), :] for x in qrow]
            k = [d_ref[pl.ds(base(p, 1) + x, 2 * Q_BLOCK), :] for x in krow]
            v = [d_ref[pl.ds(base(p, 2) + x, 2 * Q_BLOCK), :] for x in krow]
            s = [lax.dot_general(q[j], k[j], NT_DIMS, preferred_element_type=F32) + bias_ref[first[j]]
                 for j in range(ATTN_UNROLL)]
            m = [jnp.max(x, axis=-1, keepdims=True) for x in s]
            e = [jnp.exp(s[j] - m[j]) for j in range(ATTN_UNROLL)]
            l = [jnp.sum(x, axis=-1, keepdims=True) for x in e]
            acc = [jnp.dot(e[j].astype(BF16), v[j], preferred_element_type=F32) for j in range(ATTN_UNROLL)]
            for j in range(ATTN_UNROLL):
                ob_ref[rows(p * seq + tok[j], Q_BLOCK), :] = acc[j] / l[j]
                lb_ref[rows(p * seq + tok[j], Q_BLOCK), :] = jnp.broadcast_to(
                    m[j] + jnp.log(l[j]), (Q_BLOCK, HEAD_DIM))
            return carry

        lax.fori_loop(0, nblocks // ATTN_UNROLL, group, 0)

    def mix(c, carry):
        r0 = pl.multiple_of(c * rc, rc)
        lses = [lb_ref[pl.ds(p * seq + r0, rc), :] for p in range(nb)]
        mx = functools.reduce(jnp.maximum, lses)
        ws = [jnp.exp(x - mx) for x in lses]
        num = functools.reduce(lambda a, b: a + b,
                               [w * ob_ref[pl.ds(p * seq + r0, rc), :] for p, w in enumerate(ws)])
        den = functools.reduce(lambda a, b: a + b, ws)
        o_ref[0, pl.ds(r0, rc), :] = (num / den).astype(o_ref.dtype)
        return carry

    lax.fori_loop(0, seq // rc, mix, 0)


def _attention(proj, pos_b, invf, sgn, qg, kg):
    b, seq, _ = proj.shape
    ha = N_ATTN_HEADS
    blk = lambda off: pl.BlockSpec((1, seq, HEAD_DIM), lambda i, h: (i, 0, off + h))
    vec = pl.BlockSpec((1, HEAD_DIM), lambda i, h: (0, 0))
    nd = len(DILATIONS)
    assert seq % (ATTN_UNROLL * Q_BLOCK) == 0 and seq % ATTN_ROWS == 0
    return pl.pallas_call(
        functools.partial(_attn_kernel, seq=seq),
        grid=(b, ha),
        in_specs=[pl.BlockSpec((1, seq, HEAD_DIM), lambda i, h: (i, 0, 0)),
                  vec, vec, vec, vec, blk(0), blk(ha), blk(2 * ha)],
        out_specs=pl.BlockSpec((1, seq, HEAD_DIM), lambda i, h: (i, 0, h)),
        out_shape=jax.ShapeDtypeStruct((b, seq, ATTN_WIDTH), BF16),
        scratch_shapes=[pltpu.VMEM((seq, HEAD_DIM), F32),
                        pltpu.VMEM((seq, HEAD_DIM), F32),
                        pltpu.VMEM((2, seq, HEAD_DIM), F32),
                        pltpu.VMEM((3 * nd * seq, HEAD_DIM), BF16),
                        pltpu.VMEM((nd * seq, HEAD_DIM), F32),
                        pltpu.VMEM((nd * seq, HEAD_DIM), F32),
                        pltpu.VMEM((2, Q_BLOCK, 2 * Q_BLOCK), F32)],
        compiler_params=_cparams(("parallel", "arbitrary")),
        name="dilated_attn",
    )(pos_b, invf, sgn, qg, kg, proj, proj, proj)


def _softplus(x):
    return jnp.maximum(x, 0.0) + jnp.log1p(jnp.exp(-jnp.abs(x)))


def _delta_prep_kernel(ab_ref, alog_ref, dtb_ref, sel_ref, selr_ref, cwq_ref, cwk_ref, cwv_ref,
                       q_ref, k_ref, v_ref, hq_ref, hk_ref, hv_ref,
                       u_ref, w_ref, qd_ref, kt_ref, in_ref, dec_ref,
                       xp_ref, comp_ref, *, ts):
    i = pl.program_id(1)
    h = pl.program_id(2)
    npair = ts // PAIR
    ci = lax.broadcasted_iota(jnp.int32, (PAIR, PAIR), 0)
    cj = lax.broadcasted_iota(jnp.int32, (PAIR, PAIR), 1)
    same = (ci // CHUNK) == (cj // CHUNK)
    causal = same & (ci >= cj)
    strict = same & (ci > cj)
    eye = (ci == cj).astype(F32)

    @pl.when(h == 0)
    def _():
        lane = lax.broadcasted_iota(jnp.int32, (PAIR, LANES), 1)
        tril = causal.astype(F32)
        for p in range(npair):
            rs = slice(p * PAIR, (p + 1) * PAIR)
            ab = ab_ref[0, rs, :]
            g = -jnp.exp(alog_ref[...]) * _softplus(ab + dtb_ref[...])
            gc = jnp.dot(tril, g, preferred_element_type=F32, precision=lax.Precision.HIGHEST)
            comp = jnp.where(lane < N_DELTA_HEADS, gc, jax.nn.sigmoid(ab))
            for t, part in enumerate(_split_bf16(comp, 3)):
                comp_ref[rs, t * LANES:(t + 1) * LANES] = part

    def conv_silu(x_ref, halo_ref, cw_ref):
        halo = halo_ref[0]
        xp_ref[0:SUBLANES, :] = jnp.where(i > 0, halo, jnp.zeros_like(halo))
        xp_ref[SUBLANES:SUBLANES + ts, :] = x_ref[0]
        y = jnp.zeros((ts, HEAD_DIM), F32)
        for j in range(CONV_WIDTH):
            off = SUBLANES - (CONV_WIDTH - 1) + j
            y = y + cw_ref[j:j + 1, :] * xp_ref[off:off + ts, :]
        return y * jax.nn.sigmoid(y)

    ones2 = jnp.ones((2 * HEAD_DIM, HEAD_DIM), BF16)

    def l2n(x):
        ss = jnp.dot(jnp.concatenate(_split_bf16(x * x, 2), axis=1), ones2, preferred_element_type=F32)
        return x * lax.rsqrt(ss + NORM_EPS)

    q = l2n(conv_silu(q_ref, hq_ref, cwq_ref)) * (HEAD_DIM ** -0.5)
    k = l2n(conv_silu(k_ref, hk_ref, cwk_ref))
    v = conv_silu(v_ref, hv_ref, cwv_ref)
    rep = jnp.dot(comp_ref[...], sel_ref[0], preferred_element_type=F32)

    merge_masks = []
    s = 1
    while s < CHUNK:
        merge_masks.append(((ci // s) % 2 == 1) & ((cj // s) == (ci // s) - 1))
        s *= 2
    row = lax.broadcasted_iota(jnp.int32, (PAIR, LANES), 0)

    pairs = range(npair)
    rsl = [slice(p * PAIR, (p + 1) * PAIR) for p in pairs]
    gc = [rep[rs, :LANES] for rs in rsl]
    beta = [rep[rs, LANES:] for rs in rsl]
    grow = [lax.dot_general(selr_ref[0], comp_ref[rs, :], NT_DIMS, preferred_element_type=F32)
            for rs in rsl]
    gamma = [jnp.exp(jnp.where(causal, gc[p] - grow[p], NEG_INF)) for p in pairs]
    kb = [k[rsl[p]] * beta[p] for p in pairs]
    kbf = [k[rs].astype(BF16) for rs in rsl]
    kk = [lax.dot_general(kb[p].astype(BF16), kbf[p], NT_DIMS, preferred_element_type=F32) for p in pairs]
    a = [jnp.where(strict, kk[p] * gamma[p], 0.0) for p in pairs]
    t = [eye - jnp.where(merge_masks[0], a[p], 0.0) for p in pairs]
    for off in merge_masks[1:]:
        tb = [t[p].astype(BF16) for p in pairs]
        x = [jnp.dot(jnp.where(off, a[p], 0.0).astype(BF16), tb[p], preferred_element_type=F32).astype(BF16)
             for p in pairs]
        t = [t[p] - jnp.dot(tb[p], x[p], preferred_element_type=F32) for p in pairs]
    eg = [jnp.exp(gc[p]) for p in pairs]
    uw = [jnp.dot(t[p].astype(BF16),
                  jnp.concatenate([v[rsl[p]] * beta[p], kb[p] * eg[p]], axis=1).astype(BF16),
                  preferred_element_type=F32) for p in pairs]
    for p in pairs:
        rs = rsl[p]
        intra = lax.dot_general(q[rs].astype(BF16), kbf[p], NT_DIMS, preferred_element_type=F32) * gamma[p]
        g0 = gc[p][CHUNK - 1:CHUNK, :]
        g1 = gc[p][PAIR - 1:PAIR, :]
        glast = jnp.where(row < CHUNK, g0, g1)
        u_ref[0, rs, :] = uw[p][:, :HEAD_DIM]
        w_ref[0, rs, :] = uw[p][:, HEAD_DIM:].astype(BF16)
        qd_ref[0, rs, :] = (q[rs] * eg[p]).astype(BF16)
        kt_ref[0, rs, :] = (k[rs] * jnp.exp(glast - gc[p])).astype(BF16)
        in_ref[0, rs, :] = intra.astype(BF16)
        dec_ref[0, 0, 2 * p:2 * p + 1, :] = jnp.exp(g0)
        dec_ref[0, 0, 2 * p + 1:2 * p + 2, :] = jnp.exp(g1)


def _delta_prep(proj, ab, alog_vec, dtb_vec, sel, selr, conv_w):
    b, seq, _ = proj.shape
    hd = N_DELTA_HEADS
    ts = _pick(seq, (2048, 1024, 512, 256, 128))
    q0 = 3 * N_ATTN_HEADS
    tile = lambda off: pl.BlockSpec((1, ts, HEAD_DIM), lambda bi, i, h: (bi, i, off + h))
    halo = lambda off: pl.BlockSpec(
        (1, SUBLANES, HEAD_DIM),
        lambda bi, i, h: (bi, jnp.maximum(i * (ts // SUBLANES) - 1, 0), off + h))
    cw = lambda off: pl.BlockSpec((CONV_WIDTH, HEAD_DIM), lambda bi, i, h: (0, off + h))
    vec = pl.BlockSpec((1, LANES), lambda bi, i, h: (0, 0))
    out_tile = pl.BlockSpec((1, ts, HEAD_DIM), lambda bi, i, h: (bi, i, h))
    big = lambda dt: jax.ShapeDtypeStruct((b, seq, DELTA_WIDTH), dt)
    return pl.pallas_call(
        functools.partial(_delta_prep_kernel, ts=ts),
        grid=(b, seq // ts, hd),
        in_specs=[pl.BlockSpec((1, ts, LANES), lambda bi, i, h: (bi, i, 0)),
                  vec, vec,
                  pl.BlockSpec((1, 3 * LANES, 2 * LANES), lambda bi, i, h: (h, 0, 0)),
                  pl.BlockSpec((1, PAIR, 3 * LANES), lambda bi, i, h: (h, 0, 0)),
                  cw(0), cw(hd), cw(2 * hd),
                  tile(q0), tile(q0 + hd), tile(q0 + 2 * hd),
                  halo(q0), halo(q0 + hd), halo(q0 + 2 * hd)],
        out_specs=[out_tile, out_tile, out_tile, out_tile, out_tile,
                   pl.BlockSpec((1, 1, ts // CHUNK, HEAD_DIM), lambda bi, i, h: (bi, h, i, 0))],
        out_shape=[big(F32), big(BF16), big(BF16), big(BF16), big(BF16),
                   jax.ShapeDtypeStruct((b, hd, seq // CHUNK, HEAD_DIM), F32)],
        scratch_shapes=[pltpu.VMEM((SUBLANES + ts, HEAD_DIM), F32),
                        pltpu.VMEM((ts, 3 * LANES), BF16)],
        compiler_params=_cparams(("parallel", "parallel", "arbitrary")),
        name="delta_prep",
    )(ab, alog_vec, dtb_vec, sel, selr, conv_w, conv_w, conv_w, proj, proj, proj, proj, proj, proj)


def _selectors():
    hd = N_DELTA_HEADS
    src = jnp.arange(3 * LANES) % LANES
    dst = jnp.arange(2 * LANES) // LANES
    head = jnp.arange(hd)[:, None, None]
    sel = (src[None, :, None] == head + hd * dst[None, None, :]).astype(BF16)
    selr = jnp.broadcast_to((src[None, None, :] == head).astype(BF16), (hd, PAIR, 3 * LANES))
    return sel, selr


def _delta_scan_kernel(u_ref, w_ref, qd_ref, kt_ref, in_ref, dec_ref, z_ref, gain_ref, o_ref, s_ref,
                       *, nb, npair):
    @pl.when(pl.program_id(0) == 0)
    def _():
        s_ref[...] = jnp.zeros_like(s_ref)

    gain = gain_ref[...]

    chains = [(bi, h) for bi in range(nb) for h in range(N_DELTA_HEADS)]
    cols = lambda h: slice(h * HEAD_DIM, (h + 1) * HEAD_DIM)

    def pair(pi, carry):
        for e in range(2):
            sl = pl.ds(pl.multiple_of(pi * PAIR + e * CHUNK, CHUNK), CHUNK)
            st = [s_ref[bi * N_DELTA_HEADS + h] for bi, h in chains]
            sb = [s.astype(BF16) for s in st]
            ws = [jnp.dot(w_ref[bi, sl, cols(h)], sb[n], preferred_element_type=F32)
                  for n, (bi, h) in enumerate(chains)]
            qs = [jnp.dot(qd_ref[bi, sl, cols(h)], sb[n], preferred_element_type=F32)
                  for n, (bi, h) in enumerate(chains)]
            vb = [(u_ref[bi, sl, cols(h)] - ws[n]).astype(BF16) for n, (bi, h) in enumerate(chains)]
            zero = jnp.zeros((CHUNK, HEAD_DIM), BF16)
            vpair = [jnp.concatenate([x, zero] if e == 0 else [zero, x], axis=0) for x in vb]
            o = [qs[n] + jnp.dot(in_ref[bi, sl, cols(h)], vpair[n], preferred_element_type=F32)
                 for n, (bi, h) in enumerate(chains)]
            for n, (bi, h) in enumerate(chains):
                dec = dec_ref[bi, h, pl.ds(2 * pi + e, 1), :]
                s_ref[bi * N_DELTA_HEADS + h] = st[n] * dec + lax.dot_general(
                    kt_ref[bi, sl, cols(h)], vb[n], TN_DIMS, preferred_element_type=F32)
            for n, (bi, h) in enumerate(chains):
                y = o[n] * lax.rsqrt(jnp.mean(o[n] * o[n], axis=-1, keepdims=True) + NORM_EPS) * gain
                z = z_ref[bi, sl, cols(h)]
                o_ref[bi, sl, cols(h)] = (y * (z * jax.nn.sigmoid(z))).astype(o_ref.dtype)
        return carry

    lax.fori_loop(0, npair, pair, 0)


def _delta_scan(u, w, qd, kt, intra, dec, proj, gain):
    b, seq, _ = u.shape
    cg = SUBLANES
    rows = cg * CHUNK
    zblk = (3 * ATTN_WIDTH + 3 * DELTA_WIDTH) // DELTA_WIDTH
    tile = pl.BlockSpec((b, rows, DELTA_WIDTH), lambda i: (0, i, 0))
    return pl.pallas_call(
        functools.partial(_delta_scan_kernel, nb=b, npair=rows // PAIR),
        grid=(seq // rows,),
        in_specs=[tile, tile, tile, tile, tile,
                  pl.BlockSpec((b, N_DELTA_HEADS, cg, HEAD_DIM), lambda i: (0, 0, i, 0)),
                  pl.BlockSpec((b, rows, DELTA_WIDTH), lambda i: (0, i, zblk)),
                  pl.BlockSpec((1, HEAD_DIM), lambda i: (0, 0))],
        out_specs=tile,
        out_shape=jax.ShapeDtypeStruct((b, seq, DELTA_WIDTH), BF16),
        scratch_shapes=[pltpu.VMEM((b * N_DELTA_HEADS, HEAD_DIM, HEAD_DIM), F32)],
        compiler_params=_cparams(("arbitrary",)),
        name="delta_scan",
    )(u, w, qd, kt, intra, dec, proj, gain)


def _outproj_kernel(oa_ref, od_ref, wa_ref, wd_ref, x_ref, mod_ref, o_ref):
    y = (jnp.dot(oa_ref[...], wa_ref[...].astype(BF16), preferred_element_type=F32)
         + jnp.dot(od_ref[...], wd_ref[...].astype(BF16), preferred_element_type=F32))
    o_ref[...] = x_ref[...] + mod_ref[0, 5:6, :] * y


def _outproj(oa, od, w_out, x2d, mod, layer, *, seq):
    t, d = x2d.shape
    tm = _pick(seq, (512, 256, 128))
    tn = _pick(d, (2048, 1024, 512, 256, 128))
    assert ATTN_WIDTH == DELTA_WIDTH
    return pl.pallas_call(
        _outproj_kernel,
        grid=(d // tn, t // tm),
        in_specs=[pl.BlockSpec((tm, ATTN_WIDTH), lambda j, i: (i, 0)),
                  pl.BlockSpec((tm, DELTA_WIDTH), lambda j, i: (i, 0)),
                  pl.BlockSpec((None, ATTN_WIDTH, tn), lambda j, i: (layer, 0, j)),
                  pl.BlockSpec((None, DELTA_WIDTH, tn), lambda j, i: (layer, 1, j)),
                  pl.BlockSpec((tm, tn), lambda j, i: (i, j)),
                  pl.BlockSpec((1, N_MOD, tn), lambda j, i: ((i * tm) // seq, 0, j))],
        out_specs=pl.BlockSpec((tm, tn), lambda j, i: (i, j)),
        out_shape=jax.ShapeDtypeStruct((t, d), F32),
        compiler_params=_cparams(("parallel", "parallel")),
        name="outproj",
    )(oa, od, w_out, w_out, x2d, mod)


def _layer(layer, x, mod, positions, ffn1_norm, ffn1_w_gate, ffn1_w_up, ffn1_w_down, mix_norm, w_in, conv_w,
           q_norm, k_norm, a_log, dt_bias, delta_out_norm, w_out, ffn2_norm, ffn2_w_gate, ffn2_w_up,
           ffn2_w_down):
    b, seq, d = x.shape
    t = b * seq
    x2d = x.reshape(t, d)
    row = lambda v: v.reshape(1, -1).astype(F32)

    x2d = _ffn(x2d, mod, row(ffn1_norm), ffn1_w_gate, ffn1_w_up, ffn1_w_down, layer, sub=0, seq=seq)

    n_ab = w_in.shape[2] - MAIN_PROJ
    w_ab = jnp.pad(w_in[layer, :, MAIN_PROJ:], ((0, 0), (0, LANES - n_ab))).astype(BF16)
    proj, ab = _inproj(x2d, mod, row(mix_norm), jnp.swapaxes(w_in, 1, 2), w_ab, layer, seq=seq)
    proj = proj.reshape(b, seq, MAIN_PROJ)
    ab = ab.reshape(b, seq, LANES)

    lane = jnp.arange(HEAD_DIM)
    inv_freq = ROPE_THETA ** (-(lane % ROPE_HALF).astype(F32) / ROPE_HALF)
    invf = jnp.where(lane < 2 * ROPE_HALF, inv_freq, 0.0).reshape(1, HEAD_DIM).astype(F32)
    sgn = jnp.where(lane < ROPE_HALF, -1.0, jnp.where(lane < 2 * ROPE_HALF, 1.0, 0.0)).reshape(1, HEAD_DIM)
    pos_b = jnp.broadcast_to(positions.astype(F32)[..., None], (b, seq, HEAD_DIM))
    oa = _attention(proj, pos_b, invf, sgn.astype(F32), row(q_norm), row(k_norm))

    lane_vec = lambda v: jnp.pad(v.astype(F32), (0, LANES - N_DELTA_HEADS)).reshape(1, LANES)
    sel, selr = _selectors()
    u, w, qd, kt, intra, dec = _delta_prep(proj, ab, lane_vec(a_log), lane_vec(dt_bias), sel, selr,
                                           conv_w.astype(F32))
    od = _delta_scan(u, w, qd, kt, intra, dec, proj, row(delta_out_norm))

    x2d = _outproj(oa.reshape(t, ATTN_WIDTH), od.reshape(t, DELTA_WIDTH), w_out, x2d, mod, layer, seq=seq)

    x2d = _ffn(x2d, mod, row(ffn2_norm), ffn2_w_gate, ffn2_w_up, ffn2_w_down, layer, sub=2, seq=seq)
    return x2d.reshape(b, seq, d)


def kernel(x, c, positions, w_ada, b_ada, ffn1_norm, ffn1_w_gate, ffn1_w_up, ffn1_w_down, mix_norm, w_in,
           conv_w, q_norm, k_norm, a_log, dt_bias, delta_out_norm, w_out, ffn2_norm, ffn2_w_gate, ffn2_w_up,
           ffn2_w_down):
    b, _, d = x.shape
    depth = w_ada.shape[0]
    c8 = jnp.pad(c.astype(F32), ((0, SUBLANES - b), (0, 0)))
    for l in range(depth):
        mod = _adaln(c8, w_ada[l], b_ada[l].reshape(1, -1))[:b].reshape(b, N_MOD, d)
        x = _layer(l, x, mod, positions, ffn1_norm[l], ffn1_w_gate, ffn1_w_up, ffn1_w_down,
                   mix_norm[l], w_in, conv_w[l], q_norm[l], k_norm[l], a_log[l], dt_bias[l],
                   delta_out_norm[l], w_out, ffn2_norm[l], ffn2_w_gate, ffn2_w_up, ffn2_w_down)
    return x
```
